```python
import math
import jax, jax.numpy as jnp
from jax import lax
import numpy as np

D_MODEL = 2048
BATCH = 4
SEQ = 4096
DEPTH = 2

GRID_W = 64
CTX_LEN = 256
HEAD_DIM = 128
MIX_WIDTH = D_MODEL
N_MIXERS = 4
GROUP_HEADS = MIX_WIDTH // (N_MIXERS * HEAD_DIM)
NA_HEADS = GROUP_HEADS
NA_KH = 8
NA_KW = 16
GQ_Q_HEADS = GROUP_HEADS
GQ_KV_HEADS = GROUP_HEADS // 2
DF_HEADS = GROUP_HEADS
DF_QK_DIM = HEAD_DIM // 2
SW_Q_HEADS = GROUP_HEADS
SW_KV_HEADS = GROUP_HEADS // 2
SW_WINDOW = 128
BLOCK = 128
SW_NSIDE = -(-SW_WINDOW // BLOCK)
D_FF = ((8 * D_MODEL // 3 + 255) // 256) * 256
ROPE_THETA = 10000.0
EPS = 1e-6
NEG_INF = -1e30
IN_SIZES = (
    NA_HEADS * HEAD_DIM, NA_HEADS * HEAD_DIM, NA_HEADS * HEAD_DIM,
    GQ_Q_HEADS * HEAD_DIM, GQ_KV_HEADS * HEAD_DIM, GQ_KV_HEADS * HEAD_DIM,
    DF_HEADS * 2 * DF_QK_DIM, DF_HEADS * 2 * DF_QK_DIM, DF_HEADS * HEAD_DIM,
    SW_Q_HEADS * HEAD_DIM, SW_KV_HEADS * HEAD_DIM, SW_KV_HEADS * HEAD_DIM,
)
IN_WIDTH = sum(IN_SIZES)

kernel_name = 'hybrid_parallel_heads_diffusion_block'


def _rms_norm(x, g):
    xf = x.astype(jnp.float32)
    y = xf * lax.rsqrt(jnp.mean(jnp.square(xf), axis=-1, keepdims=True) + EPS)
    return (y * g.astype(jnp.float32)).astype(x.dtype)


def _modulate(x, shift, scale):
    return x * (1 + scale) + shift


def _swiglu(x, w_gate_up, w_down):
    g, u = jnp.split(x @ w_gate_up, 2, axis=-1)
    return (jax.nn.silu(g) * u) @ w_down


def _heads(t, n, d=HEAD_DIM):
    b, s, _ = t.shape
    return t.reshape(b, s, n, d).transpose(0, 2, 1, 3)


def _diff_heads(t):
    b, s, _ = t.shape
    return t.reshape(b, s, DF_HEADS, 2, DF_QK_DIM).transpose(0, 2, 1, 3, 4)


def _merge(t):
    b, h, s, d = t.shape
    return t.transpose(0, 2, 1, 3).reshape(b, s, h * d)


def _rope_tables(n, dim):
    q = dim // 4
    t = jnp.arange(n)
    pos = jnp.stack([t // GRID_W, t % GRID_W], axis=-1).astype(jnp.float32)
    inv = ROPE_THETA ** (-jnp.arange(q, dtype=jnp.float32) / q)
    ang = pos[:, :, None] * inv
    return jnp.cos(ang), jnp.sin(ang)


def _apply_rope(x, cos, sin):
    q = x.shape[-1] // 4
    xs = x.astype(jnp.float32).reshape(x.shape[:-1] + (2, 2, q))
    bshape = (x.shape[2],) + (1,) * (x.ndim - 4) + (2, q)
    cos = cos.reshape(bshape)
    sin = sin.reshape(bshape)
    x1, x2 = xs[..., 0, :], xs[..., 1, :]
    out = jnp.stack([x1 * cos - x2 * sin, x2 * cos + x1 * sin], axis=-2)
    return out.reshape(x.shape).astype(x.dtype)


def _softmax_attend(q, k, v, sink=None):
    b, hq, nq, d = q.shape
    hkv = k.shape[1]
    g = hq // hkv
    qg = q.reshape(b, hkv, g, nq, d)
    s = jnp.einsum('bkgqd,bktd->bkgqt', qg, k).astype(jnp.float32) * (d ** -0.5)
    if sink is not None:
        sk = jnp.broadcast_to(sink.astype(jnp.float32).reshape(1, hkv, g, 1, 1), s.shape[:-1] + (1,))
        p = jax.nn.softmax(jnp.concatenate([s, sk], axis=-1), axis=-1)[..., :-1]
    else:
        p = jax.nn.softmax(s, axis=-1)
    o = jnp.einsum('bkgqt,bktd->bkgqd', p.astype(v.dtype), v)
    return o.reshape(b, hq, nq, d)


def _sweep_query_blocks(attend, q):
    b, h, s = q.shape[:3]
    nb = s // BLOCK
    qb = jnp.moveaxis(q.reshape((b, h, nb, BLOCK) + q.shape[3:]), 2, 0)
    o = lax.map(attend, qb)
    return jnp.moveaxis(o, 0, 2).reshape(b, h, s, o.shape[-1])


def _diff_attend(q, k, v, lam):
    dq = q.shape[-1]
    s = jnp.einsum('bhqnd,bhtnd->bhnqt', q, k).astype(jnp.float32) * (dq ** -0.5)
    p = jax.nn.softmax(s, axis=-1)
    w = p[:, :, 0] - lam * p[:, :, 1]
    return jnp.einsum('bhqt,bhtd->bhqd', w.astype(v.dtype), v)


def _neighbourhood_attn(q, k, v, kc, vc, rpb):
    b, h, s, d = q.shape
    rows = s // GRID_W
    kh = min(NA_KH, rows)
    kw = NA_KW
    scale = d ** -0.5
    qg = q.reshape(b, h, rows, GRID_W, d)
    kg = k.reshape(b, h, rows, GRID_W, d)
    vg = v.reshape(b, h, rows, GRID_W, d)
    col = jnp.arange(GRID_W)
    kcol = jnp.clip(col - kw // 2, 0, GRID_W - kw)[:, None] + jnp.arange(kw)[None, :]
    dc = kcol - col[:, None] + (NA_KW - 1)
    rpb_c = rpb[:, :, dc]

    def row_block(r):
        r0 = jnp.clip(r - kh // 2, 0, rows - kh)
        kb = lax.dynamic_slice_in_dim(kg, r0, kh, axis=2)[:, :, :, kcol]
        vb = lax.dynamic_slice_in_dim(vg, r0, kh, axis=2)[:, :, :, kcol]
        qr = lax.dynamic_index_in_dim(qg, r, axis=2, keepdims=False)
        dr = r0 + jnp.arange(kh) - r + (NA_KH - 1)
        bias = jnp.transpose(rpb_c[:, dr], (0, 2, 1, 3)).astype(jnp.float32)
        s_loc = jnp.einsum('bhwd,bhiwjd->bhwij', qr, kb).astype(jnp.float32) * scale + bias
        s_ctx = jnp.einsum('bhwd,bhld->bhwl', qr, kc).astype(jnp.float32) * scale
        p = jax.nn.softmax(jnp.concatenate([s_loc.reshape(b, h, GRID_W, kh * kw), s_ctx], axis=-1), axis=-1)
        p_loc = p[..., :kh * kw].reshape(b, h, GRID_W, kh, kw).astype(v.dtype)
        p_ctx = p[..., kh * kw:].astype(vc.dtype)
        return (jnp.einsum('bhwij,bhiwjd->bhwd', p_loc, vb)
                + jnp.einsum('bhwl,bhld->bhwd', p_ctx, vc))

    o = lax.map(row_block, jnp.arange(rows))
    return jnp.moveaxis(o, 0, 2).reshape(b, h, s, d)


def _window_attn(q, k, v, kc, vc, sink):
    b, hq, s, d = q.shape
    hkv = k.shape[1]
    g = hq // hkv
    nb = s // BLOCK
    pad = SW_NSIDE * BLOCK
    band = (2 * SW_NSIDE + 1) * BLOCK
    idx = jnp.arange(nb)[:, None] + jnp.arange(2 * SW_NSIDE + 1)[None, :]

    def band_of(t):
        tp = jnp.pad(t, ((0, 0), (0, 0), (pad, pad), (0, 0))).reshape(b, hkv, nb + 2 * SW_NSIDE, BLOCK, d)
        return tp[:, :, idx].reshape(b, hkv, nb, band, d)

    kb, vb = band_of(k), band_of(v)
    qb = q.reshape(b, hkv, g, nb, BLOCK, d)
    scale = d ** -0.5
    s_loc = jnp.einsum('bkgnqd,bkntd->bkgnqt', qb, kb).astype(jnp.float32) * scale
    blk = jnp.arange(nb)[:, None, None] * BLOCK
    qpos = blk + jnp.arange(BLOCK)[None, :, None]
    kpos = blk - pad + jnp.arange(band)[None, None, :]
    ok = (kpos >= 0) & (kpos < s) & (jnp.abs(qpos - kpos) <= SW_WINDOW)
    s_loc = jnp.where(ok, s_loc, NEG_INF)
    s_ctx = jnp.einsum('bkgnqd,bkld->bkgnql', qb, kc).astype(jnp.float32) * scale
    n_ctx = kc.shape[2]
    s_sink = jnp.broadcast_to(sink.astype(jnp.float32).reshape(1, hkv, g, 1, 1, 1), s_ctx.shape[:-1] + (1,))
    p = jax.nn.softmax(jnp.concatenate([s_loc, s_ctx, s_sink], axis=-1), axis=-1)
    o = (jnp.einsum('bkgnqt,bkntd->bkgnqd', p[..., :band].astype(v.dtype), vb)
         + jnp.einsum('bkgnql,bkld->bkgnqd', p[..., band:band + n_ctx].astype(vc.dtype), vc))
    return o.reshape(b, hq, s, d)


def _token_mixers(a, ac, w_in, na_rpb, qk_norm_g, df_lambda, df_subln_g, sw_sink, lam_init, rope, rope_df, ctx_out):
    pts = [int(p) for p in np.cumsum(IN_SIZES)[:-1]]
    (na_q, na_k, na_v, gq_q, gq_k, gq_v, df_q, df_k, df_v, sw_q, sw_k, sw_v) = jnp.split(a @ w_in, pts, axis=-1)
    (na_qc, na_kc, na_vc, gq_qc, gq_kc, gq_vc, df_qc, df_kc, df_vc, sw_qc, sw_kc, sw_vc) = jnp.split(ac @ w_in, pts, axis=-1)
    cos, sin = rope
    cos_df, sin_df = rope_df

    na_kch, na_vch = _heads(na_kc, NA_HEADS), _heads(na_vc, NA_HEADS)
    y_na = _neighbourhood_attn(_heads(na_q, NA_HEADS), _heads(na_k, NA_HEADS), _heads(na_v, NA_HEADS),
                               na_kch, na_vch, na_rpb)

    gq_qh = _apply_rope(_rms_norm(_heads(gq_q, GQ_Q_HEADS), qk_norm_g[0]), cos, sin)
    gq_kh = _apply_rope(_rms_norm(_heads(gq_k, GQ_KV_HEADS), qk_norm_g[1]), cos, sin)
    gq_kch = _rms_norm(_heads(gq_kc, GQ_KV_HEADS), qk_norm_g[1])
    gq_vch = _heads(gq_vc, GQ_KV_HEADS)
    gq_kall = jnp.concatenate([gq_kch, gq_kh], axis=2)
    gq_vall = jnp.concatenate([gq_vch, _heads(gq_v, GQ_KV_HEADS)], axis=2)
    y_gq = _sweep_query_blocks(lambda qb: _softmax_attend(qb, gq_kall, gq_vall), gq_qh)

    lq1, lk1, lq2, lk2 = df_lambda.astype(jnp.float32)
    lam = jnp.exp(jnp.sum(lq1 * lk1)) - jnp.exp(jnp.sum(lq2 * lk2)) + lam_init
    df_qh = _apply_rope(_diff_heads(df_q), cos_df, sin_df)
    df_kh = _apply_rope(_diff_heads(df_k), cos_df, sin_df)
    df_kch = _diff_heads(df_kc)
    df_vch = _heads(df_vc, DF_HEADS)
    df_kall = jnp.concatenate([df_kch, df_kh], axis=2)
    df_vall = jnp.concatenate([df_vch, _heads(df_v, DF_HEADS)], axis=2)
    y_df = _sweep_query_blocks(lambda qb: _diff_attend(qb, df_kall, df_vall, lam), df_qh)
    y_df = _rms_norm(y_df, df_subln_g) * (1 - lam_init)

    sw_qh = _apply_rope(_heads(sw_q, SW_Q_HEADS), cos, sin)
    sw_kh = _apply_rope(_heads(sw_k, SW_KV_HEADS), cos, sin)
    sw_kch, sw_vch = _heads(sw_kc, SW_KV_HEADS), _heads(sw_vc, SW_KV_HEADS)
    y_sw = _window_attn(sw_qh, sw_kh, _heads(sw_v, SW_KV_HEADS), sw_kch, sw_vch, sw_sink)

    y = jnp.concatenate([_merge(y_na), _merge(y_gq), _merge(y_df), _merge(y_sw)], axis=-1)
    if not ctx_out:
        return y, None

    yc_na = _softmax_attend(_heads(na_qc, NA_HEADS), na_kch, na_vch)
    yc_gq = _softmax_attend(_rms_norm(_heads(gq_qc, GQ_Q_HEADS), qk_norm_g[0]), gq_kch, gq_vch)
    yc_df = _rms_norm(_diff_attend(_diff_heads(df_qc), df_kch, df_vch, lam), df_subln_g) * (1 - lam_init)
    yc_sw = _softmax_attend(_heads(sw_qc, SW_Q_HEADS), sw_kch, sw_vch, sw_sink)
    yc = jnp.concatenate([_merge(yc_na), _merge(yc_gq), _merge(yc_df), _merge(yc_sw)], axis=-1)
    return y, yc


def setup_inputs(seed: int = 0) -> dict:
    key = jax.random.key(seed)
    ks = jax.random.split(key, 16)
    f32 = jnp.float32
    nrm = lambda k, shape, s: jax.random.normal(k, shape, f32) * s
    return {
        'x': nrm(ks[0], (BATCH, SEQ, D_MODEL), 1.0),
        'c': nrm(ks[1], (BATCH, D_MODEL), 1.0),
        'ctx': nrm(ks[2], (BATCH, CTX_LEN, D_MODEL), 1.0),
        'c_ctx': nrm(ks[3], (D_MODEL,), 1.0),
        'w_mod': nrm(ks[4], (DEPTH, D_MODEL, 6 * D_MODEL), 0.5 * D_MODEL ** -0.5),
        'b_mod': nrm(ks[5], (DEPTH, 6 * D_MODEL), 0.01),
        'norm_g': 1.0 + nrm(ks[6], (DEPTH, 4, D_MODEL), 0.05),
        'w_in': nrm(ks[7], (DEPTH, D_MODEL, IN_WIDTH), D_MODEL ** -0.5),
        'w_out': nrm(ks[8], (DEPTH, MIX_WIDTH, D_MODEL), MIX_WIDTH ** -0.5),
        'na_rpb': nrm(ks[9], (DEPTH, NA_HEADS, 2 * NA_KH - 1, 2 * NA_KW - 1), 0.1),
        'qk_norm_g': 1.0 + nrm(ks[10], (DEPTH, 2, HEAD_DIM), 0.05),
        'df_lambda': nrm(ks[11], (DEPTH, 4, DF_QK_DIM), 0.1),
        'df_subln_g': 1.0 + nrm(ks[12], (DEPTH, HEAD_DIM), 0.05),
        'sw_sink': nrm(ks[13], (DEPTH, SW_Q_HEADS), 1.0),
        'w_gate_up': nrm(ks[14], (DEPTH, D_MODEL, 2 * D_FF), D_MODEL ** -0.5),
        'w_down': nrm(ks[15], (DEPTH, D_FF, D_MODEL), D_FF ** -0.5),
    }


def reference(x, c, ctx, c_ctx, w_mod, b_mod, norm_g, w_in, w_out, na_rpb, qk_norm_g,
              df_lambda, df_subln_g, sw_sink, w_gate_up, w_down):
    n_tok = x.shape[1]
    rope = _rope_tables(n_tok, HEAD_DIM)
    rope_df = _rope_tables(n_tok, DF_QK_DIM)
    silu_c = jax.nn.silu(c)
    silu_cc = jax.nn.silu(c_ctx)
    h, hc = x, ctx
    for l in range(DEPTH):
        ctx_out = l < DEPTH - 1
        m = jnp.split((silu_c @ w_mod[l] + b_mod[l])[:, None, :], 6, axis=-1)
        mc = jnp.split(silu_cc @ w_mod[l] + b_mod[l], 6, axis=-1)
        lam_init = 0.8 - 0.6 * math.exp(-0.3 * l)
        a = _modulate(_rms_norm(h, norm_g[l, 0]), m[0], m[1])
        ac = _modulate(_rms_norm(hc, norm_g[l, 0]), mc[0], mc[1])
        y, yc = _token_mixers(a, ac, w_in[l], na_rpb[l], qk_norm_g[l], df_lambda[l], df_subln_g[l],
                              sw_sink[l], lam_init, rope, rope_df, ctx_out)
        h = h + m[2] * _rms_norm(y @ w_out[l], norm_g[l, 1])
        f = _modulate(_rms_norm(h, norm_g[l, 2]), m[3], m[4])
        h = h + m[5] * _rms_norm(_swiglu(f, w_gate_up[l], w_down[l]), norm_g[l, 3])
        if ctx_out:
            hc = hc + mc[2] * _rms_norm(yc @ w_out[l], norm_g[l, 1])
            fc = _modulate(_rms_norm(hc, norm_g[l, 2]), mc[3], mc[4])
            hc = hc + mc[5] * _rms_norm(_swiglu(fc, w_gate_up[l], w_down[l]), norm_g[l, 3])
    return h
```

```python
import functools
import math

import jax
import jax.numpy as jnp
import numpy as np
from jax import lax
from jax.experimental import pallas as pl
from jax.experimental.pallas import tpu as pltpu

D_MODEL = 2048
BATCH = 4
SEQ = 4096
DEPTH = 2
GRID_W = 64
GRID_ROWS = SEQ // GRID_W
CTX_LEN = 256
HEAD_DIM = 128
NA_KH = 8
NA_KW = 16
DF_QK_DIM = 64
SW_WINDOW = 128
D_FF = 5632
IN_WIDTH = 5120
ROPE_THETA = 10000.0
EPS = 1e-6
NEG_INF = -1e30

NA_Q, NA_K, NA_V = 0, 4, 8
GQ_Q, GQ_K, GQ_V = 12, 16, 18
DF_Q, DF_K, DF_V = 20, 24, 28
SW_Q, SW_K, SW_V = 32, 36, 38

F32 = jnp.float32
BF16 = jnp.bfloat16
MIB = 1024 * 1024

TM_PROJ = 512
TF_FFN = 512
TN_MOD = 1536
TQ_GQ = 512
TQ_DF = 256
NA_QROWS = 4
NA_KROWS = 12
TQ_NA = NA_QROWS * GRID_W
TK_NA = NA_KROWS * GRID_W
TQ_SW = 256
TK_SW = TQ_SW + 2 * SW_WINDOW


def _params(vmem_mib):
    return pltpu.CompilerParams(vmem_limit_bytes=vmem_mib * MIB)


def _resident(shape):
    zeros = (0,) * len(shape)
    return pl.BlockSpec(shape, lambda *_: zeros, pipeline_mode=pl.Buffered(1))


def _rms(x):
    return x * lax.rsqrt(jnp.mean(x * x, axis=-1, keepdims=True) + EPS)


def _dot(a, b):
    return jnp.dot(a, b, preferred_element_type=F32)


def _dot_nt(a, b):
    return lax.dot_general(a, b, (((1,), (1,)), ((), ())), preferred_element_type=F32)


def _mod_kernel(c_ref, w_ref, b_ref, o_ref):
    c = c_ref[...]
    s = c * jax.nn.sigmoid(c)
    o_ref[0] = jnp.dot(s, w_ref[0], preferred_element_type=F32,
                       precision=lax.Precision.HIGHEST) + b_ref[0]


def _modulation(cvec, w_mod, b_mod):
    n = 6 * D_MODEL
    return pl.pallas_call(
        _mod_kernel,
        out_shape=jax.ShapeDtypeStruct((DEPTH, 8, n), F32),
        grid=(DEPTH, n // TN_MOD),
        in_specs=[
            pl.BlockSpec((8, D_MODEL), lambda l, j: (0, 0)),
            pl.BlockSpec((1, D_MODEL, TN_MOD), lambda l, j: (l, 0, j)),
            pl.BlockSpec((1, 1, TN_MOD), lambda l, j: (l, 0, j)),
        ],
        out_specs=pl.BlockSpec((1, 8, TN_MOD), lambda l, j: (l, 0, j)),
        compiler_params=_params(40),
        name="modulation",
    )(cvec, w_mod, b_mod.reshape(DEPTH, 1, n))


def _swap_halves(x, half):
    lane = lax.broadcasted_iota(jnp.int32, x.shape, 1)
    fwd = pltpu.roll(x, HEAD_DIM - half, axis=1)
    bwd = pltpu.roll(x, half, axis=1)
    return jnp.where((lane & (2 * half - 1)) < half, fwd, bwd)


def _inproj_kernel(*refs, rope):
    if rope:
        h_ref, mod_ref, g_ref, qkg_ref, w_ref, rope_ref, o_ref, a_scr = refs
    else:
        h_ref, mod_ref, g_ref, qkg_ref, w_ref, o_ref, a_scr = refs
        rope_ref = None
    xn = _rms(h_ref[...]) * g_ref[...]
    a = xn * (1.0 + mod_ref[0, 1:2, :]) + mod_ref[0, 0:1, :]
    a_scr[...] = a.astype(BF16)

    def rot(x, table, half):
        if rope_ref is None:
            return x
        return x * rope_ref[table] + _swap_halves(x, half) * rope_ref[table + 1]

    scale = HEAD_DIM ** -0.5
    scale_df = DF_QK_DIM ** -0.5
    plain = lambda x, j: x
    groups = [
        (NA_Q, 4, lambda x, j: x * scale),
        (NA_K, 4, plain),
        (NA_V, 4, plain),
        (GQ_Q, 4, lambda x, j: rot(_rms(x) * qkg_ref[0:1, :], 0, 32) * scale),
        (GQ_K, 2, lambda x, j: rot(_rms(x) * qkg_ref[1:2, :], 0, 32)),
        (GQ_V, 2, plain),
        (DF_Q, 4, lambda x, j: rot(x, 2, 16) * scale_df),
        (DF_K, 4, lambda x, j: rot(x, 2, 16)),
        (DF_V, 4, plain),
        (SW_Q, 4, lambda x, j: rot(x, 0, 32) * scale),
        (SW_K, 2, lambda x, j: rot(x, 0, 32)),
        (SW_V, 2, plain),
    ]
    for first, count, fn in groups:
        c0 = first * HEAD_DIM
        acc = _dot(a_scr[...], w_ref[:, c0:c0 + count * HEAD_DIM])
        for j in range(count):
            x = fn(acc[:, j * HEAD_DIM:(j + 1) * HEAD_DIM], j)
            o_ref[:, c0 + j * HEAD_DIM:c0 + (j + 1) * HEAD_DIM] = x.astype(BF16)


def _in_projection(h, mod, mod_row, g, qkg, w_in, rope_tab):
    rows = h.shape[0]
    tm = TM_PROJ
    rope = rope_tab is not None
    in_specs = [
        pl.BlockSpec((tm, D_MODEL), lambda i: (i, 0)),
        pl.BlockSpec((1, 6, D_MODEL), lambda i: (mod_row(i), 0, 0)),
        pl.BlockSpec((1, D_MODEL), lambda i: (0, 0)),
        pl.BlockSpec((2, HEAD_DIM), lambda i: (0, 0)),
        _resident((D_MODEL, IN_WIDTH)),
    ]
    args = [h, mod, g, qkg, w_in]
    if rope:
        tiles_per_seq = SEQ // tm
        in_specs.append(pl.BlockSpec((4, tm, HEAD_DIM), lambda i: (0, i % tiles_per_seq, 0)))
        args.append(rope_tab)
    return pl.pallas_call(
        functools.partial(_inproj_kernel, rope=rope),
        out_shape=jax.ShapeDtypeStruct((rows, IN_WIDTH), BF16),
        grid=(rows // tm,),
        in_specs=in_specs,
        out_specs=pl.BlockSpec((tm, IN_WIDTH), lambda i: (i, 0)),
        scratch_shapes=[pltpu.VMEM((tm, D_MODEL), BF16)],
        compiler_params=_params(56),
        name="in_projection",
    )(*args)


def _softmax_pv(scores, values, extra_logit=None):
    m = None
    for s in scores:
        mi = jnp.max(s, axis=-1, keepdims=True)
        m = mi if m is None else jnp.maximum(m, mi)
    if extra_logit is not None:
        m = jnp.maximum(m, extra_logit)
    l = None
    o = None
    for s, v in zip(scores, values):
        p = jnp.exp(s - m)
        li = jnp.sum(p, axis=-1, keepdims=True)
        oi = _dot(p.astype(BF16), v)
        l = li if l is None else l + li
        o = oi if o is None else o + oi
    if extra_logit is not None:
        l = l + jnp.exp(extra_logit - m)
    return o * (1.0 / l)


def _diff_lambda(lam_ref, lam_init):
    lam = lam_ref[...]
    a = jnp.sum(lam[0:1, :] * lam[1:2, :], axis=-1, keepdims=True)
    b = jnp.sum(lam[2:3, :] * lam[3:4, :], axis=-1, keepdims=True)
    return jnp.exp(a) - jnp.exp(b) + lam_init


def _diff_attend(q, keys, values, lam):
    lane = lax.broadcasted_iota(jnp.int32, q.shape, 1)
    zero = jnp.zeros_like(q)
    halves = (jnp.where(lane < DF_QK_DIM, q, zero), jnp.where(lane >= DF_QK_DIM, q, zero))
    exps, norms = [], []
    for qh in halves:
        scores = [_dot_nt(qh, k) for k in keys]
        m = None
        for s in scores:
            mi = jnp.max(s, axis=-1, keepdims=True)
            m = mi if m is None else jnp.maximum(m, mi)
        e = [jnp.exp(s - m) for s in scores]
        l = None
        for ei in e:
            li = jnp.sum(ei, axis=-1, keepdims=True)
            l = li if l is None else l + li
        exps.append(e)
        norms.append(1.0 / l)
    o = None
    for i, v in enumerate(values):
        w = exps[0][i] * norms[0] - exps[1][i] * (lam * norms[1])
        oi = _dot(w.astype(BF16), v)
        o = oi if o is None else o + oi
    return o


def _na_kernel(q_ref, kl_ref, vl_ref, kc_ref, vc_ref, bias_ref, o_ref):
    t = pl.program_id(2)
    first_row = jnp.clip(NA_QROWS * t - NA_KH // 2, 0, GRID_ROWS - NA_KROWS)
    start = pl.multiple_of(first_row * GRID_W, TQ_NA)
    q = q_ref[...]
    s_loc = _dot_nt(q, kl_ref[pl.ds(start, TK_NA), :]) + bias_ref[0, 0]
    s_ctx = _dot_nt(q, kc_ref[...])
    o = _softmax_pv([s_loc, s_ctx], [vl_ref[pl.ds(start, TK_NA), :], vc_ref[...]])
    o_ref[...] = o.astype(BF16)


def _na_bias_table(rpb):
    tables = []
    for t in (0, 1, GRID_ROWS // NA_QROWS - 1):
        first_row = min(max(NA_QROWS * t - NA_KH // 2, 0), GRID_ROWS - NA_KROWS)
        r = NA_QROWS * t + np.arange(NA_QROWS)
        kr = first_row + np.arange(NA_KROWS)
        c = np.arange(GRID_W)
        kc = np.arange(GRID_W)
        r0 = np.clip(r - NA_KH // 2, 0, GRID_ROWS - NA_KH)
        c0 = np.clip(c - NA_KW // 2, 0, GRID_W - NA_KW)
        ok_r = (kr[None, :] >= r0[:, None]) & (kr[None, :] < r0[:, None] + NA_KH)
        ok_c = (kc[None, :] >= c0[:, None]) & (kc[None, :] < c0[:, None] + NA_KW)
        dr = np.clip(kr[None, :] - r[:, None] + NA_KH - 1, 0, 2 * NA_KH - 2)
        dc = np.clip(kc[None, :] - c[:, None] + NA_KW - 1, 0, 2 * NA_KW - 2)
        vals = rpb[:, dr[:, None, :, None], dc[None, :, None, :]]
        ok = ok_r[:, None, :, None] & ok_c[None, :, None, :]
        tables.append(jnp.where(ok[None], vals.astype(F32), NEG_INF).reshape(4, TQ_NA, TK_NA))
    return jnp.stack(tables)


def _na_attention(qkv, qkv_c, bias):
    nt = SEQ // TQ_NA
    last = nt - 1

    def bias_idx(b, h, t):
        return (jnp.where(t == 0, 0, jnp.where(t == last, 2, 1)), h, 0, 0)

    kv = lambda col: pl.BlockSpec((SEQ, HEAD_DIM), lambda b, h, t: (b, col + h))
    kvc = lambda col: pl.BlockSpec((CTX_LEN, HEAD_DIM), lambda b, h, t: (b, col + h))
    return pl.pallas_call(
        _na_kernel,
        out_shape=jax.ShapeDtypeStruct((BATCH * SEQ, 4 * HEAD_DIM), BF16),
        grid=(BATCH, 4, nt),
        in_specs=[
            pl.BlockSpec((TQ_NA, HEAD_DIM), lambda b, h, t: (b * nt + t, NA_Q + h)),
            kv(NA_K), kv(NA_V), kvc(NA_K), kvc(NA_V),
            pl.BlockSpec((1, 1, TQ_NA, TK_NA), bias_idx),
        ],
        out_specs=pl.BlockSpec((TQ_NA, HEAD_DIM), lambda b, h, t: (b * nt + t, h)),
        compiler_params=_params(32),
        name="na_attention",
    )(qkv, qkv, qkv, qkv_c, qkv_c, bias)


def _gq_kernel(q_ref, kl_ref, vl_ref, kc_ref, vc_ref, o_ref):
    for g in range(2):
        q = q_ref[:, g * HEAD_DIM:(g + 1) * HEAD_DIM]
        scores = [_dot_nt(q, kc_ref[...]), _dot_nt(q, kl_ref[...])]
        o = _softmax_pv(scores, [vc_ref[...], vl_ref[...]])
        o_ref[:, g * HEAD_DIM:(g + 1) * HEAD_DIM] = o.astype(BF16)


def _gq_attention(qkv, qkv_c):
    nt = SEQ // TQ_GQ
    kv = lambda col: pl.BlockSpec((SEQ, HEAD_DIM), lambda b, k, t: (b, col + k))
    kvc = lambda col: pl.BlockSpec((CTX_LEN, HEAD_DIM), lambda b, k, t: (b, col + k))
    return pl.pallas_call(
        _gq_kernel,
        out_shape=jax.ShapeDtypeStruct((BATCH * SEQ, 4 * HEAD_DIM), BF16),
        grid=(BATCH, 2, nt),
        in_specs=[
            pl.BlockSpec((TQ_GQ, 2 * HEAD_DIM), lambda b, k, t: (b * nt + t, GQ_Q // 2 + k)),
            kv(GQ_K), kv(GQ_V), kvc(GQ_K), kvc(GQ_V),
        ],
        out_specs=pl.BlockSpec((TQ_GQ, 2 * HEAD_DIM), lambda b, k, t: (b * nt + t, k)),
        compiler_params=_params(48),
        name="gq_attention",
    )(qkv, qkv, qkv, qkv_c, qkv_c)


def _df_kernel(lam_ref, subg_ref, q_ref, kl_ref, vl_ref, kc_ref, vc_ref, o_ref, *, lam_init):
    lam = _diff_lambda(lam_ref, lam_init)
    o = _diff_attend(q_ref[...], [kc_ref[...], kl_ref[...]], [vc_ref[...], vl_ref[...]], lam)
    o_ref[...] = (_rms(o) * subg_ref[...] * (1.0 - lam_init)).astype(BF16)


def _df_attention(qkv, qkv_c, df_lambda, subg, lam_init):
    nt = SEQ // TQ_DF
    kv = lambda col: pl.BlockSpec((SEQ, HEAD_DIM), lambda b, h, t: (b, col + h))
    kvc = lambda col: pl.BlockSpec((CTX_LEN, HEAD_DIM), lambda b, h, t: (b, col + h))
    return pl.pallas_call(
        functools.partial(_df_kernel, lam_init=lam_init),
        out_shape=jax.ShapeDtypeStruct((BATCH * SEQ, 4 * HEAD_DIM), BF16),
        grid=(BATCH, 4, nt),
        in_specs=[
            pl.BlockSpec((4, DF_QK_DIM), lambda b, h, t: (0, 0)),
            pl.BlockSpec((1, HEAD_DIM), lambda b, h, t: (0, 0)),
            pl.BlockSpec((TQ_DF, HEAD_DIM), lambda b, h, t: (b * nt + t, DF_Q + h)),
            kv(DF_K), kv(DF_V), kvc(DF_K), kvc(DF_V),
        ],
        out_specs=pl.BlockSpec((TQ_DF, HEAD_DIM), lambda b, h, t: (b * nt + t, h)),
        compiler_params=_params(56),
        name="df_attention",
    )(df_lambda, subg, qkv, qkv, qkv, qkv_c, qkv_c)


def _sw_kernel(sink_ref, q_ref, kl_ref, vl_ref, kc_ref, vc_ref, o_ref):
    t = pl.program_id(2)
    start = pl.multiple_of(jnp.clip(TQ_SW * t - SW_WINDOW, 0, SEQ - TK_SW), SW_WINDOW)
    qpos = TQ_SW * t + lax.broadcasted_iota(jnp.int32, (TQ_SW, TK_SW), 0)
    kpos = start + lax.broadcasted_iota(jnp.int32, (TQ_SW, TK_SW), 1)
    ok = jnp.abs(qpos - kpos) <= SW_WINDOW
    kw = kl_ref[pl.ds(start, TK_SW), :]
    vw = vl_ref[pl.ds(start, TK_SW), :]
    for g in range(2):
        q = q_ref[:, g * HEAD_DIM:(g + 1) * HEAD_DIM]
        s_loc = jnp.where(ok, _dot_nt(q, kw), NEG_INF)
        s_ctx = _dot_nt(q, kc_ref[...])
        o = _softmax_pv([s_loc, s_ctx], [vw, vc_ref[...]], extra_logit=sink_ref[0, g:g + 1, 0:1])
        o_ref[:, g * HEAD_DIM:(g + 1) * HEAD_DIM] = o.astype(BF16)


def _sw_attention(qkv, qkv_c, sink):
    nt = SEQ // TQ_SW
    kv = lambda col: pl.BlockSpec((SEQ, HEAD_DIM), lambda b, k, t: (b, col + k))
    kvc = lambda col: pl.BlockSpec((CTX_LEN, HEAD_DIM), lambda b, k, t: (b, col + k))
    return pl.pallas_call(
        _sw_kernel,
        out_shape=jax.ShapeDtypeStruct((BATCH * SEQ, 4 * HEAD_DIM), BF16),
        grid=(BATCH, 2, nt),
        in_specs=[
            pl.BlockSpec((1, 2, HEAD_DIM), lambda b, k, t: (k, 0, 0)),
            pl.BlockSpec((TQ_SW, 2 * HEAD_DIM), lambda b, k, t: (b * nt + t, SW_Q // 2 + k)),
            kv(SW_K), kv(SW_V), kvc(SW_K), kvc(SW_V),
        ],
        out_specs=pl.BlockSpec((TQ_SW, 2 * HEAD_DIM), lambda b, k, t: (b * nt + t, k)),
        compiler_params=_params(32),
        name="sw_attention",
    )(sink.reshape(2, 2, HEAD_DIM), qkv, qkv, qkv, qkv_c, qkv_c)


def _ctx_attn_kernel(lam_ref, subg_ref, sink_ref, x_ref, o_ref, *, lam_init):
    chunk = lambda j: x_ref[:, j * HEAD_DIM:(j + 1) * HEAD_DIM]

    def put(j, o):
        o_ref[:, j * HEAD_DIM:(j + 1) * HEAD_DIM] = o.astype(BF16)

    lam = _diff_lambda(lam_ref, lam_init)
    for h in range(4):
        put(h, _softmax_pv([_dot_nt(chunk(NA_Q + h), chunk(NA_K + h))], [chunk(NA_V + h)]))
        put(4 + h, _softmax_pv([_dot_nt(chunk(GQ_Q + h), chunk(GQ_K + h // 2))],
                               [chunk(GQ_V + h // 2)]))
        o = _diff_attend(chunk(DF_Q + h), [chunk(DF_K + h)], [chunk(DF_V + h)], lam)
        put(8 + h, _rms(o) * subg_ref[...] * (1.0 - lam_init))
        put(12 + h, _softmax_pv([_dot_nt(chunk(SW_Q + h), chunk(SW_K + h // 2))],
                                [chunk(SW_V + h // 2)], extra_logit=sink_ref[h:h + 1, 0:1]))


def _ctx_attention(qkv_c, df_lambda, subg, sink, lam_init):
    return pl.pallas_call(
        functools.partial(_ctx_attn_kernel, lam_init=lam_init),
        out_shape=jax.ShapeDtypeStruct((BATCH * CTX_LEN, D_MODEL), BF16),
        grid=(BATCH,),
        in_specs=[
            pl.BlockSpec((4, DF_QK_DIM), lambda b: (0, 0)),
            pl.BlockSpec((1, HEAD_DIM), lambda b: (0, 0)),
            pl.BlockSpec((4, HEAD_DIM), lambda b: (0, 0)),
            pl.BlockSpec((CTX_LEN, IN_WIDTH), lambda b: (b, 0)),
        ],
        out_specs=pl.BlockSpec((CTX_LEN, D_MODEL), lambda b: (b, 0)),
        compiler_params=_params(32),
        name="ctx_attention",
    )(df_lambda, subg, sink, qkv_c)


def _outproj_kernel(y0_ref, y1_ref, y2_ref, y3_ref, h_ref, mod_ref, g_ref, w_ref, o_ref):
    acc = None
    for j, y_ref in enumerate((y0_ref, y1_ref, y2_ref, y3_ref)):
        part = _dot(y_ref[...], w_ref[j * 512:(j + 1) * 512, :])
        acc = part if acc is None else acc + part
    o_ref[...] = h_ref[...] + mod_ref[0, 2:3, :] * (_rms(acc) * g_ref[...])


def _out_projection(ys, y_cols, h, mod, mod_row, g, w_out):
    rows = h.shape[0]
    tm = TM_PROJ
    y_specs = [pl.BlockSpec((tm, 512), lambda i, c=c: (i, c)) for c in y_cols]
    return pl.pallas_call(
        _outproj_kernel,
        out_shape=jax.ShapeDtypeStruct((rows, D_MODEL), F32),
        grid=(rows // tm,),
        in_specs=y_specs + [
            pl.BlockSpec((tm, D_MODEL), lambda i: (i, 0)),
            pl.BlockSpec((1, 6, D_MODEL), lambda i: (mod_row(i), 0, 0)),
            pl.BlockSpec((1, D_MODEL), lambda i: (0, 0)),
            _resident((D_MODEL, D_MODEL)),
        ],
        out_specs=pl.BlockSpec((tm, D_MODEL), lambda i: (i, 0)),
        compiler_params=_params(48),
        name="out_projection",
    )(*ys, h, mod, g, w_out)


def _ffn_kernel(h_ref, mod_ref, g_ref, wg_ref, wu_ref, wd_ref, o_ref, f_scr, acc_scr):
    k = pl.program_id(1)

    @pl.when(k == 0)
    def _():
        xn = _rms(h_ref[...]) * g_ref[0:1, :]
        f_scr[...] = (xn * (1.0 + mod_ref[0, 4:5, :]) + mod_ref[0, 3:4, :]).astype(BF16)

    f = f_scr[...]
    gate = _dot(f, wg_ref[...])
    up = _dot(f, wu_ref[...])
    act = (gate * jax.nn.sigmoid(gate) * up).astype(BF16)
    part = _dot(act, wd_ref[...])

    @pl.when(k == 0)
    def _():
        acc_scr[...] = part

    @pl.when(k > 0)
    def _():
        acc_scr[...] += part

    @pl.when(k == pl.num_programs(1) - 1)
    def _():
        o_ref[...] = h_ref[...] + mod_ref[0, 5:6, :] * (_rms(acc_scr[...]) * g_ref[1:2, :])


def _ffn(h, mod, mod_row, g, w_gate_up, w_down):
    rows = h.shape[0]
    tm, tf = TM_PROJ, TF_FFN
    nf = D_FF // tf
    return pl.pallas_call(
        _ffn_kernel,
        out_shape=jax.ShapeDtypeStruct((rows, D_MODEL), F32),
        grid=(rows // tm, nf),
        in_specs=[
            pl.BlockSpec((tm, D_MODEL), lambda i, k: (i, 0)),
            pl.BlockSpec((1, 6, D_MODEL), lambda i, k: (mod_row(i), 0, 0)),
            pl.BlockSpec((2, D_MODEL), lambda i, k: (0, 0)),
            pl.BlockSpec((D_MODEL, tf), lambda i, k: (0, k)),
            pl.BlockSpec((D_MODEL, tf), lambda i, k: (0, nf + k)),
            pl.BlockSpec((tf, D_MODEL), lambda i, k: (k, 0)),
        ],
        out_specs=pl.BlockSpec((tm, D_MODEL), lambda i, k: (i, 0)),
        scratch_shapes=[pltpu.VMEM((tm, D_MODEL), BF16), pltpu.VMEM((tm, D_MODEL), F32)],
        compiler_params=_params(48),
        name="ffn",
    )(h, mod, g, w_gate_up, w_gate_up, w_down)


def _rope_table(dim):
    q = dim // 4
    t = jnp.arange(SEQ)
    pos = jnp.stack([t // GRID_W, t % GRID_W], axis=-1).astype(F32)
    inv = ROPE_THETA ** (-jnp.arange(q, dtype=F32) / q)
    ang = pos[:, :, None] * inv
    cos = jnp.broadcast_to(jnp.cos(ang)[:, :, None, :], (SEQ, 2, 2, q)).reshape(SEQ, dim)
    sin = jnp.sin(ang)
    sin = jnp.stack([-sin, sin], axis=2).reshape(SEQ, dim)
    reps = HEAD_DIM // dim
    return jnp.tile(cos, (1, reps)), jnp.tile(sin, (1, reps))


def kernel(x, c, ctx, c_ctx, w_mod, b_mod, norm_g, w_in, w_out, na_rpb, qk_norm_g,
           df_lambda, df_subln_g, sw_sink, w_gate_up, w_down):
    cvec = jnp.concatenate([c_ctx[None, :], c, jnp.zeros((3, D_MODEL), F32)], axis=0)
    mod_all = _modulation(cvec, w_mod, b_mod).reshape(DEPTH, 8, 6, D_MODEL)
    rope_tab = jnp.stack(_rope_table(HEAD_DIM) + _rope_table(DF_QK_DIM))

    tiles_per_seq = SEQ // TM_PROJ
    lat_row = lambda i: 1 + i // tiles_per_seq
    ctx_row = lambda i: 0

    h = x.reshape(BATCH * SEQ, D_MODEL)
    hc = ctx.reshape(BATCH * CTX_LEN, D_MODEL)
    for l in range(DEPTH):
        ctx_out = l < DEPTH - 1
        lam_init = 0.8 - 0.6 * math.exp(-0.3 * l)
        mod = mod_all[l]
        w_in_l = w_in[l].astype(BF16)
        w_out_l = w_out[l].astype(BF16)
        w_gu_l = w_gate_up[l].astype(BF16)
        w_dn_l = w_down[l].astype(BF16)
        g = norm_g[l]
        subg = df_subln_g[l][None, :]
        sink = jnp.broadcast_to(sw_sink[l][:, None], (4, HEAD_DIM))

        qkv = _in_projection(h, mod, lat_row, g[0:1], qk_norm_g[l], w_in_l, rope_tab)
        qkv_c = _in_projection(hc, mod, ctx_row, g[0:1], qk_norm_g[l], w_in_l, None)

        ys = [
            _na_attention(qkv, qkv_c, _na_bias_table(na_rpb[l])),
            _gq_attention(qkv, qkv_c),
            _df_attention(qkv, qkv_c, df_lambda[l], subg, lam_init),
            _sw_attention(qkv, qkv_c, sink),
        ]
        h = _out_projection(ys, (0, 0, 0, 0), h, mod, lat_row, g[1:2], w_out_l)
        h = _ffn(h, mod, lat_row, g[2:4], w_gu_l, w_dn_l)
        if ctx_out:
            yc = _ctx_attention(qkv_c, df_lambda[l], subg, sink, lam_init)
            hc = _out_projection([yc] * 4, (0, 1, 2, 3), hc, mod, ctx_row, g[1:2], w_out_l)
            hc = _ffn(hc, mod, ctx_row, g[2:4], w_gu_l, w_dn_l)
    return h.reshape(BATCH, SEQ, D_MODEL)
```

```python
import functools
import math

import jax
import jax.numpy as jnp
import numpy as np
from jax import lax
from jax.experimental import pallas as pl
from jax.experimental.pallas import tpu as pltpu

D_MODEL = 2048
BATCH = 4
SEQ = 4096
DEPTH = 2
GRID_W = 64
GRID_ROWS = SEQ // GRID_W
CTX_LEN = 256
HEAD_DIM = 128
NA_KH = 8
NA_KW = 16
DF_QK_DIM = 64
SW_WINDOW = 128
D_FF = 5632
IN_WIDTH = 5120
ROPE_THETA = 10000.0
EPS = 1e-6
NEG_INF = -1e30
LOG2E = math.log2(math.e)

NA_Q, NA_K, NA_V = 0, 4, 8
GQ_Q, GQ_K, GQ_V = 12, 16, 18
DF_Q, DF_K, DF_V = 20, 24, 28
SW_Q, SW_K, SW_V = 32, 36, 38

F32 = jnp.float32
BF16 = jnp.bfloat16
MIB = 1024 * 1024

TM_PROJ = 512
ROW_SUB = 256
TM_UP = 1024
TF_UP = 512
TN_MOD = 1536
TQ_FLASH = 512
TK_FLASH = 1024
NA_QROWS = 4
NA_KROWS = 12
TQ_NA = NA_QROWS * GRID_W
TK_NA = NA_KROWS * GRID_W
TQ_SW = 256
TK_SW = TQ_SW + 2 * SW_WINDOW
LOCAL_SUB = 4


def _params(vmem_mib):
    return pltpu.CompilerParams(vmem_limit_bytes=vmem_mib * MIB)


def _resident(shape):
    zeros = (0,) * len(shape)
    return pl.BlockSpec(shape, lambda *_: zeros, pipeline_mode=pl.Buffered(1))


def _rms(x):
    return x * lax.rsqrt(jnp.mean(x * x, axis=-1, keepdims=True) + EPS)


def _dot(a, b):
    return jnp.dot(a, b, preferred_element_type=F32)


def _dot_nt(a, b):
    return lax.dot_general(a, b, (((1,), (1,)), ((), ())), preferred_element_type=F32)


def _mod_kernel(c_ref, w_ref, b_ref, o_ref):
    c = c_ref[...]
    s = c * jax.nn.sigmoid(c)
    o_ref[0] = jnp.dot(s, w_ref[0], preferred_element_type=F32,
                       precision=lax.Precision.HIGHEST) + b_ref[0]


def _modulation(cvec, w_mod, b_mod):
    n = 6 * D_MODEL
    return pl.pallas_call(
        _mod_kernel,
        out_shape=jax.ShapeDtypeStruct((DEPTH, 8, n), F32),
        grid=(DEPTH, n // TN_MOD),
        in_specs=[
            pl.BlockSpec((8, D_MODEL), lambda l, j: (0, 0)),
            pl.BlockSpec((1, D_MODEL, TN_MOD), lambda l, j: (l, 0, j)),
            pl.BlockSpec((1, 1, TN_MOD), lambda l, j: (l, 0, j)),
        ],
        out_specs=pl.BlockSpec((1, 8, TN_MOD), lambda l, j: (l, 0, j)),
        compiler_params=_params(40),
        name="modulation",
    )(cvec, w_mod, b_mod.reshape(DEPTH, 1, n))


def _swap_halves(x, half):
    lane = lax.broadcasted_iota(jnp.int32, x.shape, 1)
    fwd = pltpu.roll(x, HEAD_DIM - half, axis=1)
    bwd = pltpu.roll(x, half, axis=1)
    return jnp.where((lane & (2 * half - 1)) < half, fwd, bwd)


def _inproj_kernel(*refs, rope):
    if rope:
        h_ref, mod_ref, g_ref, qkg_ref, w_ref, rope_ref, o_ref, a_scr = refs
    else:
        h_ref, mod_ref, g_ref, qkg_ref, w_ref, o_ref, a_scr = refs
        rope_ref = None
    xn = _rms(h_ref[...]) * g_ref[0:1, :]
    a = xn * (1.0 + mod_ref[0, 1:2, :]) + mod_ref[0, 0:1, :]
    a_scr[...] = a.astype(BF16)

    def rot(x, table, half):
        if rope_ref is None:
            return x
        return x * rope_ref[table] + _swap_halves(x, half) * rope_ref[table + 1]

    scale = HEAD_DIM ** -0.5 * LOG2E
    scale_df = DF_QK_DIM ** -0.5 * LOG2E
    plain = lambda x: x
    groups = [
        (NA_Q, 4, lambda x: x * scale),
        (NA_K, 4, plain),
        (NA_V, 4, plain),
        (GQ_Q, 4, lambda x: rot(_rms(x) * qkg_ref[0:1, :], 0, 32) * scale),
        (GQ_K, 2, lambda x: rot(_rms(x) * qkg_ref[1:2, :], 0, 32)),
        (GQ_V, 2, plain),
        (DF_Q, 4, lambda x: rot(x, 2, 16) * scale_df),
        (DF_K, 4, lambda x: rot(x, 2, 16)),
        (DF_V, 4, plain),
        (SW_Q, 4, lambda x: rot(x, 0, 32) * scale),
        (SW_K, 2, lambda x: rot(x, 0, 32)),
        (SW_V, 2, plain),
    ]
    for first, count, fn in groups:
        c0 = first * HEAD_DIM
        acc = _dot(a_scr[...], w_ref[:, c0:c0 + count * HEAD_DIM])
        for j in range(count):
            x = fn(acc[:, j * HEAD_DIM:(j + 1) * HEAD_DIM])
            o_ref[:, c0 + j * HEAD_DIM:c0 + (j + 1) * HEAD_DIM] = x.astype(BF16)


def _in_projection(h, mod, mod_row, g, qkg, w_in, rope_tab):
    rows = h.shape[0]
    tm = TM_PROJ
    rope = rope_tab is not None
    in_specs = [
        pl.BlockSpec((tm, D_MODEL), lambda i: (i, 0)),
        pl.BlockSpec((1, 6, D_MODEL), lambda i: (mod_row(i), 0, 0)),
        pl.BlockSpec((4, D_MODEL), lambda i: (0, 0)),
        pl.BlockSpec((2, HEAD_DIM), lambda i: (0, 0)),
        _resident((D_MODEL, IN_WIDTH)),
    ]
    args = [h, mod, g, qkg, w_in]
    if rope:
        tiles_per_seq = SEQ // tm
        in_specs.append(pl.BlockSpec((4, tm, HEAD_DIM), lambda i: (0, i % tiles_per_seq, 0)))
        args.append(rope_tab)
    return pl.pallas_call(
        functools.partial(_inproj_kernel, rope=rope),
        out_shape=jax.ShapeDtypeStruct((rows, IN_WIDTH), BF16),
        grid=(rows // tm,),
        in_specs=in_specs,
        out_specs=pl.BlockSpec((tm, IN_WIDTH), lambda i: (i, 0)),
        scratch_shapes=[pltpu.VMEM((tm, D_MODEL), BF16)],
        compiler_params=_params(56),
        name="in_projection",
    )(*args)


def _softmax_pv(scores, values, extra_logit=None):
    m = None
    for s in scores:
        mi = jnp.max(s, axis=-1, keepdims=True)
        m = mi if m is None else jnp.maximum(m, mi)
    if extra_logit is not None:
        m = jnp.maximum(m, extra_logit)
    l = None
    o = None
    for s, v in zip(scores, values):
        p = jnp.exp2(s - m)
        li = jnp.sum(p, axis=-1, keepdims=True)
        oi = _dot(p.astype(BF16), v)
        l = li if l is None else l + li
        o = oi if o is None else o + oi
    if extra_logit is not None:
        l = l + jnp.exp2(extra_logit - m)
    return o * (1.0 / l)


def _online_softmax_pv(q, key_chunks, value_chunks):
    n = len(key_chunks)
    s_next = _dot_nt(q, key_chunks[0]())
    m = None
    acc = None
    for i in range(n):
        s = s_next
        if i + 1 < n:
            s_next = _dot_nt(q, key_chunks[i + 1]())
        mi = jnp.max(s, axis=-1, keepdims=True)
        m_new = mi if m is None else jnp.maximum(m, mi)
        pv = _dot(jnp.exp2(s - m_new).astype(BF16), value_chunks[i]())
        acc = pv if acc is None else acc * jnp.exp2(m - m_new) + pv
        m = m_new
    return acc


def _fill_value_ones(vaug, vc_ref, vl_ref):
    vaug[0:CTX_LEN, 0:HEAD_DIM] = vc_ref[...]
    vaug[CTX_LEN:, 0:HEAD_DIM] = vl_ref[...]
    vaug[:, HEAD_DIM:] = jnp.ones((CTX_LEN + SEQ, HEAD_DIM), BF16)


def _flash_chunks(kl_ref, kc_ref, vaug):
    keys = [lambda: kc_ref[...]]
    values = [lambda: vaug[0:CTX_LEN, :]]
    for c in range(SEQ // TK_FLASH):
        keys.append(lambda c=c: kl_ref[c * TK_FLASH:(c + 1) * TK_FLASH, :])
        values.append(lambda c=c: vaug[CTX_LEN + c * TK_FLASH:CTX_LEN + (c + 1) * TK_FLASH, :])
    return keys, values


def _diff_lambda(lam_ref, lam_init):
    lam = lam_ref[...]
    a = jnp.sum(lam[0:1, :] * lam[1:2, :], axis=-1, keepdims=True)
    b = jnp.sum(lam[2:3, :] * lam[3:4, :], axis=-1, keepdims=True)
    return jnp.exp(a) - jnp.exp(b) + lam_init


def _split_diff_query(q):
    lane = lax.broadcasted_iota(jnp.int32, q.shape, 1)
    zero = jnp.zeros_like(q)
    return jnp.concatenate([jnp.where(lane < DF_QK_DIM, q, zero),
                            jnp.where(lane >= DF_QK_DIM, q, zero)], axis=0)


def _na_kernel(q_ref, kl_ref, vl_ref, kc_ref, vc_ref, b0_ref, b1_ref, b2_ref, o_ref):
    step = pl.program_id(2)
    bias_refs = [b0_ref] + [b1_ref] * (LOCAL_SUB - 2) + [b2_ref]
    for j in range(LOCAL_SUB):
        t = LOCAL_SUB * step + j
        first_row = jnp.clip(NA_QROWS * t - NA_KH // 2, 0, GRID_ROWS - NA_KROWS)
        start = pl.multiple_of(first_row * GRID_W, TQ_NA)
        q = q_ref[j * TQ_NA:(j + 1) * TQ_NA, :]
        s_loc = _dot_nt(q, kl_ref[pl.ds(start, TK_NA), :]) + bias_refs[j][0, 0]
        s_ctx = _dot_nt(q, kc_ref[...])
        o = _softmax_pv([s_loc, s_ctx], [vl_ref[pl.ds(start, TK_NA), :], vc_ref[...]])
        o_ref[j * TQ_NA:(j + 1) * TQ_NA, :] = o.astype(BF16)


def _na_window_geometry(t):
    first_row = min(max(NA_QROWS * t - NA_KH // 2, 0), GRID_ROWS - NA_KROWS)
    r = NA_QROWS * t + np.arange(NA_QROWS)
    kr = first_row + np.arange(NA_KROWS)
    c = np.arange(GRID_W)
    kc = np.arange(GRID_W)
    r0 = np.clip(r - NA_KH // 2, 0, GRID_ROWS - NA_KH)
    c0 = np.clip(c - NA_KW // 2, 0, GRID_W - NA_KW)
    ok_r = (kr[None, :] >= r0[:, None]) & (kr[None, :] < r0[:, None] + NA_KH)
    ok_c = (kc[None, :] >= c0[:, None]) & (kc[None, :] < c0[:, None] + NA_KW)
    dr = np.clip(kr[None, :] - r[:, None] + NA_KH - 1, 0, 2 * NA_KH - 2)
    dc = np.clip(kc[None, :] - c[:, None] + NA_KW - 1, 0, 2 * NA_KW - 2)
    return dr, dc, ok_r, ok_c


def _na_bias_table(rpb):
    rpb = rpb.astype(F32) * LOG2E
    tables = []
    for t in (0, 1, GRID_ROWS // NA_QROWS - 1):
        dr, dc, ok_r, ok_c = _na_window_geometry(t)
        pick_r = (dr[:, :, None] == np.arange(2 * NA_KH - 1)).astype(np.float32)
        pick_c = (dc[:, :, None] == np.arange(2 * NA_KW - 1)).astype(np.float32)
        rows = jnp.einsum("hde,rid->hrie", rpb, pick_r, precision=lax.Precision.HIGHEST)
        vals = jnp.einsum("hrie,cke->hrcik", rows, pick_c, precision=lax.Precision.HIGHEST)
        ok = ok_r[:, None, :, None] & ok_c[None, :, None, :]
        tables.append(jnp.where(ok[None], vals, NEG_INF).reshape(4, TQ_NA, TK_NA))
    return jnp.stack(tables)


def _na_attention(qkv, qkv_c, bias):
    tq = LOCAL_SUB * TQ_NA
    nt = SEQ // tq
    kv = lambda col: pl.BlockSpec((SEQ, HEAD_DIM), lambda b, h, t: (b, col + h))
    kvc = lambda col: pl.BlockSpec((CTX_LEN, HEAD_DIM), lambda b, h, t: (b, col + h))
    bias_block = (1, 1, TQ_NA, TK_NA)
    return pl.pallas_call(
        _na_kernel,
        out_shape=jax.ShapeDtypeStruct((BATCH * SEQ, 4 * HEAD_DIM), BF16),
        grid=(BATCH, 4, nt),
        in_specs=[
            pl.BlockSpec((tq, HEAD_DIM), lambda b, h, t: (b * nt + t, NA_Q + h)),
            kv(NA_K), kv(NA_V), kvc(NA_K), kvc(NA_V),
            pl.BlockSpec(bias_block, lambda b, h, t: (jnp.where(t == 0, 0, 1), h, 0, 0)),
            pl.BlockSpec(bias_block, lambda b, h, t: (1, h, 0, 0)),
            pl.BlockSpec(bias_block, lambda b, h, t: (jnp.where(t == nt - 1, 2, 1), h, 0, 0)),
        ],
        out_specs=pl.BlockSpec((tq, HEAD_DIM), lambda b, h, t: (b * nt + t, h)),
        compiler_params=_params(40),
        name="na_attention",
    )(qkv, qkv, qkv, qkv_c, qkv_c, bias, bias, bias)


def _gq_kernel(q_ref, kl_ref, vl_ref, kc_ref, vc_ref, o_ref, vaug):
    @pl.when(pl.program_id(2) == 0)
    def _():
        _fill_value_ones(vaug, vc_ref, vl_ref)

    q = jnp.concatenate([q_ref[:, 0:HEAD_DIM], q_ref[:, HEAD_DIM:]], axis=0)
    acc = _online_softmax_pv(q, *_flash_chunks(kl_ref, kc_ref, vaug))
    o = acc[:, 0:HEAD_DIM] / acc[:, HEAD_DIM:]
    o_ref[:, 0:HEAD_DIM] = o[0:TQ_FLASH].astype(BF16)
    o_ref[:, HEAD_DIM:] = o[TQ_FLASH:].astype(BF16)


def _gq_attention(qkv, qkv_c):
    nt = SEQ // TQ_FLASH
    kv = lambda col: pl.BlockSpec((SEQ, HEAD_DIM), lambda b, k, t: (b, col + k))
    kvc = lambda col: pl.BlockSpec((CTX_LEN, HEAD_DIM), lambda b, k, t: (b, col + k))
    return pl.pallas_call(
        _gq_kernel,
        out_shape=jax.ShapeDtypeStruct((BATCH * SEQ, 4 * HEAD_DIM), BF16),
        grid=(BATCH, 2, nt),
        in_specs=[
            pl.BlockSpec((TQ_FLASH, 2 * HEAD_DIM), lambda b, k, t: (b * nt + t, GQ_Q // 2 + k)),
            kv(GQ_K), kv(GQ_V), kvc(GQ_K), kvc(GQ_V),
        ],
        out_specs=pl.BlockSpec((TQ_FLASH, 2 * HEAD_DIM), lambda b, k, t: (b * nt + t, k)),
        scratch_shapes=[pltpu.VMEM((CTX_LEN + SEQ, 2 * HEAD_DIM), BF16)],
        compiler_params=_params(48),
        name="gq_attention",
    )(qkv, qkv, qkv, qkv_c, qkv_c)


def _df_kernel(lam_ref, subg_ref, q_ref, kl_ref, vl_ref, kc_ref, vc_ref, o_ref, vaug, *, lam_init):
    @pl.when(pl.program_id(2) == 0)
    def _():
        _fill_value_ones(vaug, vc_ref, vl_ref)

    lam = _diff_lambda(lam_ref, lam_init)
    acc = _online_softmax_pv(_split_diff_query(q_ref[...]), *_flash_chunks(kl_ref, kc_ref, vaug))
    o = acc[:, 0:HEAD_DIM] / acc[:, HEAD_DIM:]
    o = o[0:TQ_FLASH] - lam * o[TQ_FLASH:]
    o_ref[...] = (_rms(o) * subg_ref[...] * (1.0 - lam_init)).astype(BF16)


def _df_attention(qkv, qkv_c, df_lambda, subg, lam_init):
    nt = SEQ // TQ_FLASH
    kv = lambda col: pl.BlockSpec((SEQ, HEAD_DIM), lambda b, h, t: (b, col + h))
    kvc = lambda col: pl.BlockSpec((CTX_LEN, HEAD_DIM), lambda b, h, t: (b, col + h))
    return pl.pallas_call(
        functools.partial(_df_kernel, lam_init=lam_init),
        out_shape=jax.ShapeDtypeStruct((BATCH * SEQ, 4 * HEAD_DIM), BF16),
        grid=(BATCH, 4, nt),
        in_specs=[
            pl.BlockSpec((4, DF_QK_DIM), lambda b, h, t: (0, 0)),
            pl.BlockSpec((1, HEAD_DIM), lambda b, h, t: (0, 0)),
            pl.BlockSpec((TQ_FLASH, HEAD_DIM), lambda b, h, t: (b * nt + t, DF_Q + h)),
            kv(DF_K), kv(DF_V), kvc(DF_K), kvc(DF_V),
        ],
        out_specs=pl.BlockSpec((TQ_FLASH, HEAD_DIM), lambda b, h, t: (b * nt + t, h)),
        scratch_shapes=[pltpu.VMEM((CTX_LEN + SEQ, 2 * HEAD_DIM), BF16)],
        compiler_params=_params(48),
        name="df_attention",
    )(df_lambda, subg, qkv, qkv, qkv, qkv_c, qkv_c)


def _sw_kernel(sink_ref, q_ref, kl_ref, vl_ref, kc_ref, vc_ref, m0_ref, m1_ref, m2_ref, o_ref):
    step = pl.program_id(2)
    mask_refs = [m0_ref] + [m1_ref] * (LOCAL_SUB - 2) + [m2_ref]
    for j in range(LOCAL_SUB):
        t = LOCAL_SUB * step + j
        start = pl.multiple_of(jnp.clip(TQ_SW * t - SW_WINDOW, 0, SEQ - TK_SW), SW_WINDOW)
        kw = kl_ref[pl.ds(start, TK_SW), :]
        vw = vl_ref[pl.ds(start, TK_SW), :]
        for g in range(2):
            q = q_ref[j * TQ_SW:(j + 1) * TQ_SW, g * HEAD_DIM:(g + 1) * HEAD_DIM]
            s_loc = _dot_nt(q, kw) + mask_refs[j][0]
            s_ctx = _dot_nt(q, kc_ref[...])
            o = _softmax_pv([s_loc, s_ctx], [vw, vc_ref[...]],
                            extra_logit=sink_ref[0, g:g + 1, 0:1] * LOG2E)
            o_ref[j * TQ_SW:(j + 1) * TQ_SW, g * HEAD_DIM:(g + 1) * HEAD_DIM] = o.astype(BF16)


def _sw_mask_table():
    r = np.arange(TQ_SW)[:, None]
    c = np.arange(TK_SW)[None, :]
    nt = SEQ // TQ_SW
    tables = []
    for t in (0, 1, nt - 1):
        start = min(max(TQ_SW * t - SW_WINDOW, 0), SEQ - TK_SW)
        ok = np.abs((TQ_SW * t + r) - (start + c)) <= SW_WINDOW
        tables.append(np.where(ok, 0.0, NEG_INF).astype(np.float32))
    return jnp.asarray(np.stack(tables))


def _sw_attention(qkv, qkv_c, sink):
    tq = LOCAL_SUB * TQ_SW
    nt = SEQ // tq
    mask = _sw_mask_table()
    kv = lambda col: pl.BlockSpec((SEQ, HEAD_DIM), lambda b, k, t: (b, col + k))
    kvc = lambda col: pl.BlockSpec((CTX_LEN, HEAD_DIM), lambda b, k, t: (b, col + k))
    mask_block = (1, TQ_SW, TK_SW)
    return pl.pallas_call(
        _sw_kernel,
        out_shape=jax.ShapeDtypeStruct((BATCH * SEQ, 4 * HEAD_DIM), BF16),
        grid=(BATCH, 2, nt),
        in_specs=[
            pl.BlockSpec((1, 2, HEAD_DIM), lambda b, k, t: (k, 0, 0)),
            pl.BlockSpec((tq, 2 * HEAD_DIM), lambda b, k, t: (b * nt + t, SW_Q // 2 + k)),
            kv(SW_K), kv(SW_V), kvc(SW_K), kvc(SW_V),
            pl.BlockSpec(mask_block, lambda b, k, t: (jnp.where(t == 0, 0, 1), 0, 0)),
            pl.BlockSpec(mask_block, lambda b, k, t: (1, 0, 0)),
            pl.BlockSpec(mask_block, lambda b, k, t: (jnp.where(t == nt - 1, 2, 1), 0, 0)),
        ],
        out_specs=pl.BlockSpec((tq, 2 * HEAD_DIM), lambda b, k, t: (b * nt + t, k)),
        compiler_params=_params(40),
        name="sw_attention",
    )(sink.reshape(2, 2, HEAD_DIM), qkv, qkv, qkv, qkv_c, qkv_c, mask, mask, mask)


def _ctx_attn_kernel(lam_ref, subg_ref, sink_ref, x_ref, o_ref, *, lam_init):
    chunk = lambda j: x_ref[:, j * HEAD_DIM:(j + 1) * HEAD_DIM]

    def put(j, o):
        o_ref[:, j * HEAD_DIM:(j + 1) * HEAD_DIM] = o.astype(BF16)

    lam = _diff_lambda(lam_ref, lam_init)
    for h in range(4):
        put(h, _softmax_pv([_dot_nt(chunk(NA_Q + h), chunk(NA_K + h))], [chunk(NA_V + h)]))
        put(4 + h, _softmax_pv([_dot_nt(chunk(GQ_Q + h), chunk(GQ_K + h // 2))],
                               [chunk(GQ_V + h // 2)]))
        q2 = _split_diff_query(chunk(DF_Q + h))
        o = _softmax_pv([_dot_nt(q2, chunk(DF_K + h))], [chunk(DF_V + h)])
        o = o[0:CTX_LEN] - lam * o[CTX_LEN:]
        put(8 + h, _rms(o) * subg_ref[...] * (1.0 - lam_init))
        put(12 + h, _softmax_pv([_dot_nt(chunk(SW_Q + h), chunk(SW_K + h // 2))],
                                [chunk(SW_V + h // 2)],
                                extra_logit=sink_ref[h:h + 1, 0:1] * LOG2E))


def _ctx_attention(qkv_c, df_lambda, subg, sink, lam_init):
    return pl.pallas_call(
        functools.partial(_ctx_attn_kernel, lam_init=lam_init),
        out_shape=jax.ShapeDtypeStruct((BATCH * CTX_LEN, D_MODEL), BF16),
        grid=(BATCH,),
        in_specs=[
            pl.BlockSpec((4, DF_QK_DIM), lambda b: (0, 0)),
            pl.BlockSpec((1, HEAD_DIM), lambda b: (0, 0)),
            pl.BlockSpec((4, HEAD_DIM), lambda b: (0, 0)),
            pl.BlockSpec((CTX_LEN, IN_WIDTH), lambda b: (b, 0)),
        ],
        out_specs=pl.BlockSpec((CTX_LEN, D_MODEL), lambda b: (b, 0)),
        compiler_params=_params(32),
        name="ctx_attention",
    )(df_lambda, subg, sink, qkv_c)


def _outproj_kernel(y0_ref, y1_ref, y2_ref, y3_ref, h_ref, mod_ref, g_ref, w_ref, o_ref, f_ref):
    for c in range(TM_PROJ // ROW_SUB):
        rows = slice(c * ROW_SUB, (c + 1) * ROW_SUB)
        acc = None
        for j, y_ref in enumerate((y0_ref, y1_ref, y2_ref, y3_ref)):
            part = _dot(y_ref[rows, :], w_ref[j * 512:(j + 1) * 512, :])
            acc = part if acc is None else acc + part
        h_new = h_ref[rows, :] + mod_ref[0, 2:3, :] * (_rms(acc) * g_ref[1:2, :])
        o_ref[rows, :] = h_new
        f = _rms(h_new) * g_ref[2:3, :] * (1.0 + mod_ref[0, 4:5, :]) + mod_ref[0, 3:4, :]
        f_ref[rows, :] = f.astype(BF16)


def _out_projection(ys, y_cols, h, mod, mod_row, g, w_out):
    rows = h.shape[0]
    tm = TM_PROJ
    y_specs = [pl.BlockSpec((tm, 512), lambda i, c=c: (i, c)) for c in y_cols]
    row_block = pl.BlockSpec((tm, D_MODEL), lambda i: (i, 0))
    return pl.pallas_call(
        _outproj_kernel,
        out_shape=(jax.ShapeDtypeStruct((rows, D_MODEL), F32),
                   jax.ShapeDtypeStruct((rows, D_MODEL), BF16)),
        grid=(rows // tm,),
        in_specs=y_specs + [
            row_block,
            pl.BlockSpec((1, 6, D_MODEL), lambda i: (mod_row(i), 0, 0)),
            pl.BlockSpec((4, D_MODEL), lambda i: (0, 0)),
            _resident((D_MODEL, D_MODEL)),
        ],
        out_specs=(row_block, row_block),
        compiler_params=_params(48),
        name="out_projection",
    )(*ys, h, mod, g, w_out)


def _ffn_up_kernel(f_ref, wg_ref, wu_ref, o_ref):
    f = f_ref[...]
    gate = _dot(f, wg_ref[...])
    up = _dot(f, wu_ref[...])
    o_ref[...] = (gate * jax.nn.sigmoid(gate) * up).astype(BF16)


def _ffn_up(f, w_gate_up):
    rows = f.shape[0]
    tm, tf = TM_UP, TF_UP
    nf = D_FF // tf
    return pl.pallas_call(
        _ffn_up_kernel,
        out_shape=jax.ShapeDtypeStruct((rows, D_FF), BF16),
        grid=(rows // tm, nf),
        in_specs=[
            pl.BlockSpec((tm, D_MODEL), lambda i, k: (i, 0)),
            pl.BlockSpec((D_MODEL, tf), lambda i, k: (0, k)),
            pl.BlockSpec((D_MODEL, tf), lambda i, k: (0, nf + k)),
        ],
        out_specs=pl.BlockSpec((tm, tf), lambda i, k: (i, k)),
        compiler_params=_params(40),
        name="ffn_up",
    )(f, w_gate_up, w_gate_up)


def _ffn_down_kernel(a_ref, h_ref, mod_ref, g_ref, w_ref, o_ref):
    for c in range(TM_PROJ // ROW_SUB):
        rows = slice(c * ROW_SUB, (c + 1) * ROW_SUB)
        acc = _dot(a_ref[rows, :], w_ref[...])
        o_ref[rows, :] = h_ref[rows, :] + mod_ref[0, 5:6, :] * (_rms(acc) * g_ref[3:4, :])


def _ffn_down(act, h, mod, mod_row, g, w_down):
    rows = h.shape[0]
    tm = TM_PROJ
    row_block = pl.BlockSpec((tm, D_MODEL), lambda i: (i, 0))
    return pl.pallas_call(
        _ffn_down_kernel,
        out_shape=jax.ShapeDtypeStruct((rows, D_MODEL), F32),
        grid=(rows // tm,),
        in_specs=[
            pl.BlockSpec((tm, D_FF), lambda i: (i, 0)),
            row_block,
            pl.BlockSpec((1, 6, D_MODEL), lambda i: (mod_row(i), 0, 0)),
            pl.BlockSpec((4, D_MODEL), lambda i: (0, 0)),
            _resident((D_FF, D_MODEL)),
        ],
        out_specs=row_block,
        compiler_params=_params(58),
        name="ffn_down",
    )(act, h, mod, g, w_down)


def _rope_table(dim):
    q = dim // 4
    t = jnp.arange(SEQ)
    pos = jnp.stack([t // GRID_W, t % GRID_W], axis=-1).astype(F32)
    inv = ROPE_THETA ** (-jnp.arange(q, dtype=F32) / q)
    ang = pos[:, :, None] * inv
    cos = jnp.broadcast_to(jnp.cos(ang)[:, :, None, :], (SEQ, 2, 2, q)).reshape(SEQ, dim)
    sin = jnp.sin(ang)
    sin = jnp.stack([-sin, sin], axis=2).reshape(SEQ, dim)
    reps = HEAD_DIM // dim
    return jnp.tile(cos, (1, reps)), jnp.tile(sin, (1, reps))


def kernel(x, c, ctx, c_ctx, w_mod, b_mod, norm_g, w_in, w_out, na_rpb, qk_norm_g,
           df_lambda, df_subln_g, sw_sink, w_gate_up, w_down):
    cvec = jnp.concatenate([c_ctx[None, :], c, jnp.zeros((3, D_MODEL), F32)], axis=0)
    mod_all = _modulation(cvec, w_mod, b_mod).reshape(DEPTH, 8, 6, D_MODEL)
    rope_tab = jnp.stack(_rope_table(HEAD_DIM) + _rope_table(DF_QK_DIM))

    tiles_per_seq = SEQ // TM_PROJ
    lat_row = lambda i: 1 + i // tiles_per_seq
    ctx_row = lambda i: 0

    h = x.reshape(BATCH * SEQ, D_MODEL)
    hc = ctx.reshape(BATCH * CTX_LEN, D_MODEL)
    for l in range(DEPTH):
        ctx_out = l < DEPTH - 1
        lam_init = 0.8 - 0.6 * math.exp(-0.3 * l)
        mod = mod_all[l]
        w_in_l = w_in[l].astype(BF16)
        w_out_l = w_out[l].astype(BF16)
        w_gu_l = w_gate_up[l].astype(BF16)
        w_dn_l = w_down[l].astype(BF16)
        g = norm_g[l]
        subg = df_subln_g[l][None, :]
        sink = jnp.broadcast_to(sw_sink[l][:, None], (4, HEAD_DIM))

        qkv = _in_projection(h, mod, lat_row, g, qk_norm_g[l], w_in_l, rope_tab)
        qkv_c = _in_projection(hc, mod, ctx_row, g, qk_norm_g[l], w_in_l, None)

        ys = [
            _na_attention(qkv, qkv_c, _na_bias_table(na_rpb[l])),
            _gq_attention(qkv, qkv_c),
            _df_attention(qkv, qkv_c, df_lambda[l], subg, lam_init),
            _sw_attention(qkv, qkv_c, sink),
        ]
        h, f = _out_projection(ys, (0, 0, 0, 0), h, mod, lat_row, g, w_out_l)
        h = _ffn_down(_ffn_up(f, w_gu_l), h, mod, lat_row, g, w_dn_l)
        if ctx_out:
            yc = _ctx_attention(qkv_c, df_lambda[l], subg, sink, lam_init)
            hc, fc = _out_projection([yc] * 4, (0, 1, 2, 3), hc, mod, ctx_row, g, w_out_l)
            hc = _ffn_down(_ffn_up(fc, w_gu_l), hc, mod, ctx_row, g, w_dn_l)
    return h.reshape(BATCH, SEQ, D_MODEL)
```

```python
import functools
import math

import jax
import jax.numpy as jnp
import numpy as np
from jax import lax
from jax.experimental import pallas as pl
from jax.experimental.pallas import tpu as pltpu

D_MODEL = 2048
BATCH = 4
SEQ = 4096
DEPTH = 2
GRID_W = 64
GRID_ROWS = SEQ // GRID_W
CTX_LEN = 256
HEAD_DIM = 128
NA_KH = 8
NA_KW = 16
DF_QK_DIM = 64
SW_WINDOW = 128
D_FF = 5632
IN_WIDTH = 5120
ROPE_THETA = 10000.0
EPS = 1e-6
NEG_INF = -1e30
LOG2E = math.log2(math.e)

NA_Q, NA_K, NA_V = 0, 4, 8
GQ_Q, GQ_K, GQ_V = 12, 16, 18
DF_Q, DF_K, DF_V = 20, 24, 28
SW_Q, SW_K, SW_V = 32, 36, 38

F32 = jnp.float32
BF16 = jnp.bfloat16
MIB = 1024 * 1024

TM_PROJ = 512
ROW_SUB = 256
TM_UP = 1024
TF_UP = 512
TN_MOD = 1536
TQ_FLASH = 512
FLASH_CHUNKS = (512, 1024, 1024, 1024, 512)
assert sum(FLASH_CHUNKS) == SEQ
NA_QROWS = 4
NA_KROWS = 12
TQ_NA = NA_QROWS * GRID_W
TK_NA = NA_KROWS * GRID_W
TQ_SW = 256
TK_SW = TQ_SW + 2 * SW_WINDOW
LOCAL_SUB = 4


def _params(vmem_mib):
    return pltpu.CompilerParams(vmem_limit_bytes=vmem_mib * MIB)


def _resident_layer(layer, shape):
    index = (layer,) + (0,) * len(shape)
    return pl.BlockSpec((None,) + tuple(shape), lambda *_: index, pipeline_mode=pl.Buffered(1))


def _gain(g, scale=None):
    return g if scale is None else g * (1.0 + scale)


def _rms(x):
    return x * lax.rsqrt(jnp.mean(x * x, axis=-1, keepdims=True) + EPS)


def _dot(a, b):
    return jnp.dot(a, b, preferred_element_type=F32)


def _dot_nt(a, b):
    return lax.dot_general(a, b, (((1,), (1,)), ((), ())), preferred_element_type=F32)


def _mod_kernel(c_ref, w_ref, b_ref, o_ref):
    c = c_ref[...]
    s = c * jax.nn.sigmoid(c)
    o_ref[0] = jnp.dot(s, w_ref[0], preferred_element_type=F32,
                       precision=lax.Precision.HIGHEST) + b_ref[0]


def _modulation(cvec, w_mod, b_mod):
    n = 6 * D_MODEL
    return pl.pallas_call(
        _mod_kernel,
        out_shape=jax.ShapeDtypeStruct((DEPTH, 8, n), F32),
        grid=(DEPTH, n // TN_MOD),
        in_specs=[
            pl.BlockSpec((8, D_MODEL), lambda l, j: (0, 0)),
            pl.BlockSpec((1, D_MODEL, TN_MOD), lambda l, j: (l, 0, j)),
            pl.BlockSpec((1, 1, TN_MOD), lambda l, j: (l, 0, j)),
        ],
        out_specs=pl.BlockSpec((1, 8, TN_MOD), lambda l, j: (l, 0, j)),
        compiler_params=_params(40),
        name="modulation",
    )(cvec, w_mod, b_mod.reshape(DEPTH, 1, n))


def _swap_halves(x, half):
    lane = lax.broadcasted_iota(jnp.int32, x.shape, 1)
    fwd = pltpu.roll(x, HEAD_DIM - half, axis=1)
    bwd = pltpu.roll(x, half, axis=1)
    return jnp.where((lane & (2 * half - 1)) < half, fwd, bwd)


def _inproj_kernel(*refs, rope):
    if rope:
        h_ref, mod_ref, g_ref, qkg_ref, w_ref, rope_ref, o_ref, a_scr = refs
    else:
        h_ref, mod_ref, g_ref, qkg_ref, w_ref, o_ref, a_scr = refs
        rope_ref = None
    a = _rms(h_ref[...]) * _gain(g_ref[0:1, :], mod_ref[0, 1:2, :]) + mod_ref[0, 0:1, :]
    a_scr[...] = a.astype(BF16)

    def rot(x, table, half):
        if rope_ref is None:
            return x
        return x * rope_ref[table] + _swap_halves(x, half) * rope_ref[table + 1]

    scale = HEAD_DIM ** -0.5 * LOG2E
    scale_df = DF_QK_DIM ** -0.5 * LOG2E
    plain = lambda x: x
    groups = [
        (NA_Q, 4, lambda x: x * scale),
        (NA_K, 4, plain),
        (NA_V, 4, plain),
        (GQ_Q, 4, lambda x: rot(_rms(x) * qkg_ref[0:1, :], 0, 32) * scale),
        (GQ_K, 2, lambda x: rot(_rms(x) * qkg_ref[1:2, :], 0, 32)),
        (GQ_V, 2, plain),
        (DF_Q, 4, lambda x: rot(x, 2, 16) * scale_df),
        (DF_K, 4, lambda x: rot(x, 2, 16)),
        (DF_V, 4, plain),
        (SW_Q, 4, lambda x: rot(x, 0, 32) * scale),
        (SW_K, 2, lambda x: rot(x, 0, 32)),
        (SW_V, 2, plain),
    ]
    for first, count, fn in groups:
        c0 = first * HEAD_DIM
        acc = _dot(a_scr[...], w_ref[:, c0:c0 + count * HEAD_DIM])
        for j in range(count):
            x = fn(acc[:, j * HEAD_DIM:(j + 1) * HEAD_DIM])
            o_ref[:, c0 + j * HEAD_DIM:c0 + (j + 1) * HEAD_DIM] = x.astype(BF16)


def _in_projection(h, mod, mod_row, g, qkg, w_in, layer, rope_tab):
    rows = h.shape[0]
    tm = TM_PROJ
    rope = rope_tab is not None
    in_specs = [
        pl.BlockSpec((tm, D_MODEL), lambda i: (i, 0)),
        pl.BlockSpec((1, 6, D_MODEL), lambda i: (mod_row(i), 0, 0)),
        pl.BlockSpec((4, D_MODEL), lambda i: (0, 0)),
        pl.BlockSpec((2, HEAD_DIM), lambda i: (0, 0)),
        _resident_layer(layer, (D_MODEL, IN_WIDTH)),
    ]
    args = [h, mod, g, qkg, w_in]
    if rope:
        tiles_per_seq = SEQ // tm
        in_specs.append(pl.BlockSpec((4, tm, HEAD_DIM), lambda i: (0, i % tiles_per_seq, 0)))
        args.append(rope_tab)
    return pl.pallas_call(
        functools.partial(_inproj_kernel, rope=rope),
        out_shape=jax.ShapeDtypeStruct((rows, IN_WIDTH), BF16),
        grid=(rows // tm,),
        in_specs=in_specs,
        out_specs=pl.BlockSpec((tm, IN_WIDTH), lambda i: (i, 0)),
        scratch_shapes=[pltpu.VMEM((tm, D_MODEL), BF16)],
        compiler_params=_params(56),
        name="in_projection",
    )(*args)


def _softmax_pv(s, v, extra_logit=None):
    m = jnp.max(s, axis=-1, keepdims=True)
    if extra_logit is not None:
        m = jnp.maximum(m, extra_logit)
    p = jnp.exp2(s - m)
    l = jnp.sum(p, axis=-1, keepdims=True)
    if extra_logit is not None:
        l = l + jnp.exp2(extra_logit - m)
    return _dot(p.astype(BF16), v) * (1.0 / l)


def _softmax_pv_blocks(scores, values):
    m = functools.reduce(jnp.maximum, [jnp.max(s, axis=-1, keepdims=True) for s in scores])
    l = None
    o = None
    for s, v in zip(scores, values):
        p = jnp.exp2(s - m)
        li = jnp.sum(p, axis=-1, keepdims=True)
        oi = _dot(p.astype(BF16), v)
        l = li if l is None else l + li
        o = oi if o is None else o + oi
    return o * (1.0 / l)


def _rows(*blocks):
    return jnp.concatenate(blocks, axis=0)


def _online_softmax_pv(q, key_chunks, value_chunks):
    n = len(key_chunks)
    s_next = _dot_nt(q, key_chunks[0]())
    m = None
    acc = None
    for i in range(n):
        s = s_next
        if i + 1 < n:
            s_next = _dot_nt(q, key_chunks[i + 1]())
        mi = jnp.max(s, axis=-1, keepdims=True)
        m_new = mi if m is None else jnp.maximum(m, mi)
        pv = _dot(jnp.exp2(s - m_new).astype(BF16), value_chunks[i]())
        acc = pv if acc is None else acc * jnp.exp2(m - m_new) + pv
        m = m_new
    return acc


def _fill_value_ones(vaug, vc_ref, vl_ref):
    vaug[0:CTX_LEN, 0:HEAD_DIM] = vc_ref[...]
    vaug[CTX_LEN:, 0:HEAD_DIM] = vl_ref[...]
    vaug[:, HEAD_DIM:] = jnp.ones((CTX_LEN + SEQ, HEAD_DIM), BF16)


def _flash_chunks(kl_ref, kc_ref, vaug):
    keys, values = [], []
    start = 0
    for size in FLASH_CHUNKS:
        keys.append(lambda a=start, b=start + size: kl_ref[a:b, :])
        values.append(lambda a=start, b=start + size: vaug[CTX_LEN + a:CTX_LEN + b, :])
        start += size
    keys.append(lambda: kc_ref[...])
    values.append(lambda: vaug[0:CTX_LEN, :])
    return keys, values


def _diff_lambda(lam_ref, lam_init):
    lam = lam_ref[...]
    a = jnp.sum(lam[0:1, :] * lam[1:2, :], axis=-1, keepdims=True)
    b = jnp.sum(lam[2:3, :] * lam[3:4, :], axis=-1, keepdims=True)
    return jnp.exp(a) - jnp.exp(b) + lam_init


def _split_diff_query(q):
    lane = lax.broadcasted_iota(jnp.int32, q.shape, 1)
    zero = jnp.zeros_like(q)
    return jnp.concatenate([jnp.where(lane < DF_QK_DIM, q, zero),
                            jnp.where(lane >= DF_QK_DIM, q, zero)], axis=0)


def _na_kernel(q_ref, kl_ref, vl_ref, kc_ref, vc_ref, b0_ref, b1_ref, b2_ref, o_ref):
    step = pl.program_id(2)
    bias_refs = [b0_ref] + [b1_ref] * (LOCAL_SUB - 2) + [b2_ref]

    def scores(j):
        t = LOCAL_SUB * step + j
        first_row = jnp.clip(NA_QROWS * t - NA_KH // 2, 0, GRID_ROWS - NA_KROWS)
        start = pl.multiple_of(first_row * GRID_W, TQ_NA)
        q = q_ref[j * TQ_NA:(j + 1) * TQ_NA, :]
        s_loc = _dot_nt(q, kl_ref[pl.ds(start, TK_NA), :]) + bias_refs[j][0, 0]
        return start, [s_loc, _dot_nt(q, kc_ref[...])]

    nxt = scores(0)
    for j in range(LOCAL_SUB):
        start, s = nxt
        if j + 1 < LOCAL_SUB:
            nxt = scores(j + 1)
        o = _softmax_pv_blocks(s, [vl_ref[pl.ds(start, TK_NA), :], vc_ref[...]])
        o_ref[j * TQ_NA:(j + 1) * TQ_NA, :] = o.astype(BF16)


def _na_window_geometry(t):
    first_row = min(max(NA_QROWS * t - NA_KH // 2, 0), GRID_ROWS - NA_KROWS)
    r = NA_QROWS * t + np.arange(NA_QROWS)
    kr = first_row + np.arange(NA_KROWS)
    c = np.arange(GRID_W)
    kc = np.arange(GRID_W)
    r0 = np.clip(r - NA_KH // 2, 0, GRID_ROWS - NA_KH)
    c0 = np.clip(c - NA_KW // 2, 0, GRID_W - NA_KW)
    ok_r = (kr[None, :] >= r0[:, None]) & (kr[None, :] < r0[:, None] + NA_KH)
    ok_c = (kc[None, :] >= c0[:, None]) & (kc[None, :] < c0[:, None] + NA_KW)
    dr = np.clip(kr[None, :] - r[:, None] + NA_KH - 1, 0, 2 * NA_KH - 2)
    dc = np.clip(kc[None, :] - c[:, None] + NA_KW - 1, 0, 2 * NA_KW - 2)
    return dr, dc, ok_r, ok_c


def _na_bias_table(rpb):
    rpb = rpb.astype(F32) * LOG2E
    tables = []
    for t in (0, 1, GRID_ROWS // NA_QROWS - 1):
        dr, dc, ok_r, ok_c = _na_window_geometry(t)
        pick_r = (dr[:, :, None] == np.arange(2 * NA_KH - 1)).astype(np.float32)
        pick_c = (dc[:, :, None] == np.arange(2 * NA_KW - 1)).astype(np.float32)
        rows = jnp.einsum("hde,rid->hrie", rpb, pick_r, precision=lax.Precision.HIGHEST)
        vals = jnp.einsum("hrie,cke->hrcik", rows, pick_c, precision=lax.Precision.HIGHEST)
        ok = ok_r[:, None, :, None] & ok_c[None, :, None, :]
        tables.append(jnp.where(ok[None], vals, NEG_INF).reshape(4, TQ_NA, TK_NA))
    return jnp.stack(tables)


def _na_attention(qkv, qkv_c, bias):
    tq = LOCAL_SUB * TQ_NA
    nt = SEQ // tq
    kv = lambda col: pl.BlockSpec((SEQ, HEAD_DIM), lambda b, h, t: (b, col + h))
    kvc = lambda col: pl.BlockSpec((CTX_LEN, HEAD_DIM), lambda b, h, t: (b, col + h))
    bias_block = (1, 1, TQ_NA, TK_NA)
    return pl.pallas_call(
        _na_kernel,
        out_shape=jax.ShapeDtypeStruct((BATCH * SEQ, 4 * HEAD_DIM), BF16),
        grid=(BATCH, 4, nt),
        in_specs=[
            pl.BlockSpec((tq, HEAD_DIM), lambda b, h, t: (b * nt + t, NA_Q + h)),
            kv(NA_K), kv(NA_V), kvc(NA_K), kvc(NA_V),
            pl.BlockSpec(bias_block, lambda b, h, t: (jnp.where(t == 0, 0, 1), h, 0, 0)),
            pl.BlockSpec(bias_block, lambda b, h, t: (1, h, 0, 0)),
            pl.BlockSpec(bias_block, lambda b, h, t: (jnp.where(t == nt - 1, 2, 1), h, 0, 0)),
        ],
        out_specs=pl.BlockSpec((tq, HEAD_DIM), lambda b, h, t: (b * nt + t, h)),
        compiler_params=_params(40),
        name="na_attention",
    )(qkv, qkv, qkv, qkv_c, qkv_c, bias, bias, bias)


def _gq_kernel(q_ref, kl_ref, vl_ref, kc_ref, vc_ref, o_ref, vaug):
    @pl.when(pl.program_id(2) == 0)
    def _():
        _fill_value_ones(vaug, vc_ref, vl_ref)

    q = jnp.concatenate([q_ref[:, 0:HEAD_DIM], q_ref[:, HEAD_DIM:]], axis=0)
    acc = _online_softmax_pv(q, *_flash_chunks(kl_ref, kc_ref, vaug))
    o = acc[:, 0:HEAD_DIM] / acc[:, HEAD_DIM:]
    o_ref[:, 0:HEAD_DIM] = o[0:TQ_FLASH].astype(BF16)
    o_ref[:, HEAD_DIM:] = o[TQ_FLASH:].astype(BF16)


def _gq_attention(qkv, qkv_c):
    nt = SEQ // TQ_FLASH
    kv = lambda col: pl.BlockSpec((SEQ, HEAD_DIM), lambda b, k, t: (b, col + k))
    kvc = lambda col: pl.BlockSpec((CTX_LEN, HEAD_DIM), lambda b, k, t: (b, col + k))
    return pl.pallas_call(
        _gq_kernel,
        out_shape=jax.ShapeDtypeStruct((BATCH * SEQ, 4 * HEAD_DIM), BF16),
        grid=(BATCH, 2, nt),
        in_specs=[
            pl.BlockSpec((TQ_FLASH, 2 * HEAD_DIM), lambda b, k, t: (b * nt + t, GQ_Q // 2 + k)),
            kv(GQ_K), kv(GQ_V), kvc(GQ_K), kvc(GQ_V),
        ],
        out_specs=pl.BlockSpec((TQ_FLASH, 2 * HEAD_DIM), lambda b, k, t: (b * nt + t, k)),
        scratch_shapes=[pltpu.VMEM((CTX_LEN + SEQ, 2 * HEAD_DIM), BF16)],
        compiler_params=_params(48),
        name="gq_attention",
    )(qkv, qkv, qkv, qkv_c, qkv_c)


def _df_kernel(lam_ref, subg_ref, q_ref, kl_ref, vl_ref, kc_ref, vc_ref, o_ref, vaug, *, lam_init):
    @pl.when(pl.program_id(2) == 0)
    def _():
        _fill_value_ones(vaug, vc_ref, vl_ref)

    lam = _diff_lambda(lam_ref, lam_init)
    acc = _online_softmax_pv(_split_diff_query(q_ref[...]), *_flash_chunks(kl_ref, kc_ref, vaug))
    o = acc[:, 0:HEAD_DIM] / acc[:, HEAD_DIM:]
    o = o[0:TQ_FLASH] - lam * o[TQ_FLASH:]
    o_ref[...] = (_rms(o) * (subg_ref[...] * (1.0 - lam_init))).astype(BF16)


def _df_attention(qkv, qkv_c, df_lambda, subg, lam_init):
    nt = SEQ // TQ_FLASH
    kv = lambda col: pl.BlockSpec((SEQ, HEAD_DIM), lambda b, h, t: (b, col + h))
    kvc = lambda col: pl.BlockSpec((CTX_LEN, HEAD_DIM), lambda b, h, t: (b, col + h))
    return pl.pallas_call(
        functools.partial(_df_kernel, lam_init=lam_init),
        out_shape=jax.ShapeDtypeStruct((BATCH * SEQ, 4 * HEAD_DIM), BF16),
        grid=(BATCH, 4, nt),
        in_specs=[
            pl.BlockSpec((4, DF_QK_DIM), lambda b, h, t: (0, 0)),
            pl.BlockSpec((1, HEAD_DIM), lambda b, h, t: (0, 0)),
            pl.BlockSpec((TQ_FLASH, HEAD_DIM), lambda b, h, t: (b * nt + t, DF_Q + h)),
            kv(DF_K), kv(DF_V), kvc(DF_K), kvc(DF_V),
        ],
        out_specs=pl.BlockSpec((TQ_FLASH, HEAD_DIM), lambda b, h, t: (b * nt + t, h)),
        scratch_shapes=[pltpu.VMEM((CTX_LEN + SEQ, 2 * HEAD_DIM), BF16)],
        compiler_params=_params(48),
        name="df_attention",
    )(df_lambda, subg, qkv, qkv, qkv, qkv_c, qkv_c)


def _sw_kernel(sink_ref, q_ref, kl_ref, vl_ref, kc_ref, vc_ref, m0_ref, m1_ref, m2_ref, o_ref):
    step = pl.program_id(2)
    mask_refs = [m0_ref] + [m1_ref] * (LOCAL_SUB - 2) + [m2_ref]
    head = lax.broadcasted_iota(jnp.int32, (2 * TQ_SW, 1), 0) >= TQ_SW
    sink = jnp.where(head, sink_ref[0, 1:2, 0:1], sink_ref[0, 0:1, 0:1]) * LOG2E

    for j in range(LOCAL_SUB):
        t = LOCAL_SUB * step + j
        start = pl.multiple_of(jnp.clip(TQ_SW * t - SW_WINDOW, 0, SEQ - TK_SW), SW_WINDOW)
        keys = _rows(kl_ref[pl.ds(start, TK_SW), :], kc_ref[...])
        values = _rows(vl_ref[pl.ds(start, TK_SW), :], vc_ref[...])
        rows = slice(j * TQ_SW, (j + 1) * TQ_SW)
        q = _rows(q_ref[rows, 0:HEAD_DIM], q_ref[rows, HEAD_DIM:])
        o = _softmax_pv(_dot_nt(q, keys) + mask_refs[j][0], values, extra_logit=sink)
        o_ref[rows, 0:HEAD_DIM] = o[0:TQ_SW].astype(BF16)
        o_ref[rows, HEAD_DIM:] = o[TQ_SW:].astype(BF16)


def _sw_mask_table():
    r = np.arange(TQ_SW)[:, None]
    c = np.arange(TK_SW)[None, :]
    nt = SEQ // TQ_SW
    tables = []
    for t in (0, 1, nt - 1):
        start = min(max(TQ_SW * t - SW_WINDOW, 0), SEQ - TK_SW)
        ok = np.abs((TQ_SW * t + r) - (start + c)) <= SW_WINDOW
        band = np.where(ok, 0.0, NEG_INF).astype(np.float32)
        band = np.concatenate([band, np.zeros((TQ_SW, CTX_LEN), np.float32)], axis=1)
        tables.append(np.concatenate([band, band], axis=0))
    return jnp.asarray(np.stack(tables))


def _sw_attention(qkv, qkv_c, sink):
    tq = LOCAL_SUB * TQ_SW
    nt = SEQ // tq
    mask = _sw_mask_table()
    kv = lambda col: pl.BlockSpec((SEQ, HEAD_DIM), lambda b, k, t: (b, col + k))
    kvc = lambda col: pl.BlockSpec((CTX_LEN, HEAD_DIM), lambda b, k, t: (b, col + k))
    mask_block = (1, 2 * TQ_SW, TK_SW + CTX_LEN)
    return pl.pallas_call(
        _sw_kernel,
        out_shape=jax.ShapeDtypeStruct((BATCH * SEQ, 4 * HEAD_DIM), BF16),
        grid=(BATCH, 2, nt),
        in_specs=[
            pl.BlockSpec((1, 2, HEAD_DIM), lambda b, k, t: (k, 0, 0)),
            pl.BlockSpec((tq, 2 * HEAD_DIM), lambda b, k, t: (b * nt + t, SW_Q // 2 + k)),
            kv(SW_K), kv(SW_V), kvc(SW_K), kvc(SW_V),
            pl.BlockSpec(mask_block, lambda b, k, t: (jnp.where(t == 0, 0, 1), 0, 0)),
            pl.BlockSpec(mask_block, lambda b, k, t: (1, 0, 0)),
            pl.BlockSpec(mask_block, lambda b, k, t: (jnp.where(t == nt - 1, 2, 1), 0, 0)),
        ],
        out_specs=pl.BlockSpec((tq, 2 * HEAD_DIM), lambda b, k, t: (b * nt + t, k)),
        compiler_params=_params(40),
        name="sw_attention",
    )(sink.reshape(2, 2, HEAD_DIM), qkv, qkv, qkv, qkv_c, qkv_c, mask, mask, mask)


def _ctx_attn_kernel(lam_ref, subg_ref, sink_ref, x_ref, o_ref, *, lam_init):
    chunk = lambda j: x_ref[:, j * HEAD_DIM:(j + 1) * HEAD_DIM]

    def put(j, o):
        o_ref[:, j * HEAD_DIM:(j + 1) * HEAD_DIM] = o.astype(BF16)

    lam = _diff_lambda(lam_ref, lam_init)
    for h in range(4):
        put(h, _softmax_pv(_dot_nt(chunk(NA_Q + h), chunk(NA_K + h)), chunk(NA_V + h)))
        put(4 + h, _softmax_pv(_dot_nt(chunk(GQ_Q + h), chunk(GQ_K + h // 2)),
                               chunk(GQ_V + h // 2)))
        q2 = _split_diff_query(chunk(DF_Q + h))
        o = _softmax_pv(_dot_nt(q2, chunk(DF_K + h)), chunk(DF_V + h))
        o = o[0:CTX_LEN] - lam * o[CTX_LEN:]
        put(8 + h, _rms(o) * (subg_ref[...] * (1.0 - lam_init)))
        put(12 + h, _softmax_pv(_dot_nt(chunk(SW_Q + h), chunk(SW_K + h // 2)),
                                chunk(SW_V + h // 2),
                                extra_logit=sink_ref[h:h + 1, 0:1] * LOG2E))


def _ctx_attention(qkv_c, df_lambda, subg, sink, lam_init):
    return pl.pallas_call(
        functools.partial(_ctx_attn_kernel, lam_init=lam_init),
        out_shape=jax.ShapeDtypeStruct((BATCH * CTX_LEN, D_MODEL), BF16),
        grid=(BATCH,),
        in_specs=[
            pl.BlockSpec((4, DF_QK_DIM), lambda b: (0, 0)),
            pl.BlockSpec((1, HEAD_DIM), lambda b: (0, 0)),
            pl.BlockSpec((4, HEAD_DIM), lambda b: (0, 0)),
            pl.BlockSpec((CTX_LEN, IN_WIDTH), lambda b: (b, 0)),
        ],
        out_specs=pl.BlockSpec((CTX_LEN, D_MODEL), lambda b: (b, 0)),
        compiler_params=_params(32),
        name="ctx_attention",
    )(df_lambda, subg, sink, qkv_c)


def _outproj_kernel(y0_ref, y1_ref, y2_ref, y3_ref, h_ref, mod_ref, g_ref, w_ref, o_ref, f_ref):
    gate = mod_ref[0, 2:3, :] * g_ref[1:2, :]
    ffn_gain = _gain(g_ref[2:3, :], mod_ref[0, 4:5, :])
    for c in range(TM_PROJ // ROW_SUB):
        rows = slice(c * ROW_SUB, (c + 1) * ROW_SUB)
        acc = None
        for j, y_ref in enumerate((y0_ref, y1_ref, y2_ref, y3_ref)):
            part = _dot(y_ref[rows, :], w_ref[j * 512:(j + 1) * 512, :])
            acc = part if acc is None else acc + part
        h_new = h_ref[rows, :] + _rms(acc) * gate
        o_ref[rows, :] = h_new
        f_ref[rows, :] = (_rms(h_new) * ffn_gain + mod_ref[0, 3:4, :]).astype(BF16)


def _out_projection(ys, y_cols, h, mod, mod_row, g, w_out, layer):
    rows = h.shape[0]
    tm = TM_PROJ
    y_specs = [pl.BlockSpec((tm, 512), lambda i, c=c: (i, c)) for c in y_cols]
    row_block = pl.BlockSpec((tm, D_MODEL), lambda i: (i, 0))
    return pl.pallas_call(
        _outproj_kernel,
        out_shape=(jax.ShapeDtypeStruct((rows, D_MODEL), F32),
                   jax.ShapeDtypeStruct((rows, D_MODEL), BF16)),
        grid=(rows // tm,),
        in_specs=y_specs + [
            row_block,
            pl.BlockSpec((1, 6, D_MODEL), lambda i: (mod_row(i), 0, 0)),
            pl.BlockSpec((4, D_MODEL), lambda i: (0, 0)),
            _resident_layer(layer, (D_MODEL, D_MODEL)),
        ],
        out_specs=(row_block, row_block),
        compiler_params=_params(48),
        name="out_projection",
    )(*ys, h, mod, g, w_out)


def _ffn_up_kernel(f_ref, wg_ref, wu_ref, o_ref):
    f = f_ref[...]
    gate = _dot(f, wg_ref[...].astype(BF16))
    up = _dot(f, wu_ref[...].astype(BF16))
    o_ref[...] = (gate * jax.nn.sigmoid(gate) * up).astype(BF16)


def _ffn_up(f, w_gate_up, layer):
    rows = f.shape[0]
    tm, tf = min(TM_UP, rows), TF_UP
    nf = D_FF // tf
    return pl.pallas_call(
        _ffn_up_kernel,
        out_shape=jax.ShapeDtypeStruct((rows, D_FF), BF16),
        grid=(rows // tm, nf),
        in_specs=[
            pl.BlockSpec((tm, D_MODEL), lambda i, k: (i, 0)),
            pl.BlockSpec((None, D_MODEL, tf), lambda i, k: (layer, 0, k)),
            pl.BlockSpec((None, D_MODEL, tf), lambda i, k: (layer, 0, nf + k)),
        ],
        out_specs=pl.BlockSpec((tm, tf), lambda i, k: (i, k)),
        compiler_params=_params(52),
        name="ffn_up",
    )(f, w_gate_up, w_gate_up)


def _ffn_down_kernel(a_ref, h_ref, mod_ref, g_ref, w_ref, o_ref):
    gate = mod_ref[0, 5:6, :] * g_ref[3:4, :]
    for c in range(TM_PROJ // ROW_SUB):
        rows = slice(c * ROW_SUB, (c + 1) * ROW_SUB)
        acc = _dot(a_ref[rows, :], w_ref[...])
        o_ref[rows, :] = h_ref[rows, :] + _rms(acc) * gate


def _ffn_down(act, h, mod, mod_row, g, w_down, layer):
    rows = h.shape[0]
    tm = TM_PROJ
    row_block = pl.BlockSpec((tm, D_MODEL), lambda i: (i, 0))
    return pl.pallas_call(
        _ffn_down_kernel,
        out_shape=jax.ShapeDtypeStruct((rows, D_MODEL), F32),
        grid=(rows // tm,),
        in_specs=[
            pl.BlockSpec((tm, D_FF), lambda i: (i, 0)),
            row_block,
            pl.BlockSpec((1, 6, D_MODEL), lambda i: (mod_row(i), 0, 0)),
            pl.BlockSpec((4, D_MODEL), lambda i: (0, 0)),
            _resident_layer(layer, (D_FF, D_MODEL)),
        ],
        out_specs=row_block,
        compiler_params=_params(58),
        name="ffn_down",
    )(act, h, mod, g, w_down)


def _rope_table(dim):
    q = dim // 4
    t = jnp.arange(SEQ)
    pos = jnp.stack([t // GRID_W, t % GRID_W], axis=-1).astype(F32)
    inv = ROPE_THETA ** (-jnp.arange(q, dtype=F32) / q)
    ang = pos[:, :, None] * inv
    cos = jnp.broadcast_to(jnp.cos(ang)[:, :, None, :], (SEQ, 2, 2, q)).reshape(SEQ, dim)
    sin = jnp.sin(ang)
    sin = jnp.stack([-sin, sin], axis=2).reshape(SEQ, dim)
    reps = HEAD_DIM // dim
    return jnp.tile(cos, (1, reps)), jnp.tile(sin, (1, reps))


def kernel(x, c, ctx, c_ctx, w_mod, b_mod, norm_g, w_in, w_out, na_rpb, qk_norm_g,
           df_lambda, df_subln_g, sw_sink, w_gate_up, w_down):
    cvec = jnp.concatenate([c_ctx[None, :], c, jnp.zeros((3, D_MODEL), F32)], axis=0)
    mod_all = _modulation(cvec, w_mod, b_mod).reshape(DEPTH, 8, 6, D_MODEL)
    rope_tab = jnp.stack(_rope_table(HEAD_DIM) + _rope_table(DF_QK_DIM))

    tiles_per_seq = SEQ // TM_PROJ
    lat_row = lambda i: 1 + i // tiles_per_seq
    ctx_row = lambda i: 0

    w_in, w_out, w_down = (w.astype(BF16) for w in (w_in, w_out, w_down))

    h = x.reshape(BATCH * SEQ, D_MODEL)
    hc = ctx.reshape(BATCH * CTX_LEN, D_MODEL)
    for l in range(DEPTH):
        ctx_out = l < DEPTH - 1
        lam_init = 0.8 - 0.6 * math.exp(-0.3 * l)
        mod = mod_all[l]
        g = norm_g[l]
        subg = df_subln_g[l][None, :]
        sink = jnp.broadcast_to(sw_sink[l][:, None], (4, HEAD_DIM))

        qkv = _in_projection(h, mod, lat_row, g, qk_norm_g[l], w_in, l, rope_tab)
        qkv_c = _in_projection(hc, mod, ctx_row, g, qk_norm_g[l], w_in, l, None)

        ys = [
            _na_attention(qkv, qkv_c, _na_bias_table(na_rpb[l])),
            _gq_attention(qkv, qkv_c),
            _df_attention(qkv, qkv_c, df_lambda[l], subg, lam_init),
            _sw_attention(qkv, qkv_c, sink),
        ]
        h, f = _out_projection(ys, (0, 0, 0, 0), h, mod, lat_row, g, w_out, l)
        h = _ffn_down(_ffn_up(f, w_gate_up, l), h, mod, lat_row, g, w_down, l)
        if ctx_out:
            yc = _ctx_attention(qkv_c, df_lambda[l], subg, sink, lam_init)
            hc, fc = _out_projection([yc] * 4, (0, 1, 2, 3), hc, mod, ctx_row, g, w_out, l)
            hc = _ffn_down(_ffn_up(fc, w_gate_up, l), hc, mod, ctx_row, g, w_down, l)
    return h.reshape(BATCH, SEQ, D_MODEL)
```

```python
import functools
import math

import jax
import jax.numpy as jnp
import numpy as np
from jax import lax
from jax.experimental import pallas as pl
from jax.experimental.pallas import tpu as pltpu

D_MODEL = 2048
BATCH = 4
SEQ = 4096
DEPTH = 2
GRID_W = 64
GRID_ROWS = SEQ // GRID_W
CTX_LEN = 256
HEAD_DIM = 128
NA_KH = 8
NA_KW = 16
DF_QK_DIM = 64
SW_WINDOW = 128
D_FF = 5632
IN_WIDTH = 5120
ROPE_THETA = 10000.0
EPS = 1e-6
NEG_INF = -1e30
LOG2E = math.log2(math.e)

NA_Q, NA_K, NA_V = 0, 4, 8
GQ_Q, GQ_K, GQ_V = 12, 16, 18
DF_Q, DF_K, DF_V = 20, 24, 28
SW_Q, SW_K, SW_V = 32, 36, 38

F32 = jnp.float32
BF16 = jnp.bfloat16
MIB = 1024 * 1024

TM_PROJ = 512
ROW_SUB = 256
TM_UP = 1024
TF_UP = 512
TN_MOD = 1536
TQ_FLASH = 512
FLASH_CHUNKS = (512, 1024, 1024, 1024, 512)
assert sum(FLASH_CHUNKS) == SEQ
NA_QROWS = 4
NA_KROWS = 12
TQ_NA = NA_QROWS * GRID_W
TK_NA = NA_KROWS * GRID_W
TQ_SW = 256
TK_SW = TQ_SW + 2 * SW_WINDOW
NA_SUB = 8
SW_SUB = 4


def _params(vmem_mib):
    return pltpu.CompilerParams(vmem_limit_bytes=vmem_mib * MIB)


def _resident_layer(layer, shape):
    index = (layer,) + (0,) * len(shape)
    return pl.BlockSpec((None,) + tuple(shape), lambda *_: index, pipeline_mode=pl.Buffered(1))


def _gain(g, scale=None):
    return g if scale is None else g * (1.0 + scale)


def _rms(x):
    return x * lax.rsqrt(jnp.mean(x * x, axis=-1, keepdims=True) + EPS)


def _dot(a, b):
    return jnp.dot(a, b, preferred_element_type=F32)


def _dot_nt(a, b):
    return lax.dot_general(a, b, (((1,), (1,)), ((), ())), preferred_element_type=F32)


def _mod_kernel(c_ref, w_ref, b_ref, o_ref):
    c = c_ref[...]
    s = c * jax.nn.sigmoid(c)
    hi = s.astype(BF16)
    lo = (s - hi.astype(F32)).astype(BF16)
    r = _dot(jnp.concatenate([hi, lo], axis=0), w_ref[0].astype(BF16))
    o_ref[0] = r[0:8] + r[8:16] + b_ref[0]


def _modulation(cvec, w_mod, b_mod):
    n = 6 * D_MODEL
    return pl.pallas_call(
        _mod_kernel,
        out_shape=jax.ShapeDtypeStruct((DEPTH, 8, n), F32),
        grid=(DEPTH, n // TN_MOD),
        in_specs=[
            pl.BlockSpec((8, D_MODEL), lambda l, j: (0, 0)),
            pl.BlockSpec((1, D_MODEL, TN_MOD), lambda l, j: (l, 0, j)),
            pl.BlockSpec((1, 1, TN_MOD), lambda l, j: (l, 0, j)),
        ],
        out_specs=pl.BlockSpec((1, 8, TN_MOD), lambda l, j: (l, 0, j)),
        compiler_params=_params(40),
        name="modulation",
    )(cvec, w_mod, b_mod.reshape(DEPTH, 1, n))


def _swap_halves(x, half):
    lane = lax.broadcasted_iota(jnp.int32, x.shape, 1)
    fwd = pltpu.roll(x, HEAD_DIM - half, axis=1)
    bwd = pltpu.roll(x, half, axis=1)
    return jnp.where((lane & (2 * half - 1)) < half, fwd, bwd)


def _inproj_kernel(*refs, rope):
    if rope:
        h_ref, mod_ref, g_ref, qkg_ref, w_ref, rope_ref, o_ref, a_scr = refs
    else:
        h_ref, mod_ref, g_ref, qkg_ref, w_ref, o_ref, a_scr = refs
        rope_ref = None
    a = _rms(h_ref[...]) * _gain(g_ref[0:1, :], mod_ref[0, 1:2, :]) + mod_ref[0, 0:1, :]
    a_scr[...] = a.astype(BF16)

    def rot(x, table, half):
        if rope_ref is None:
            return x
        return x * rope_ref[table] + _swap_halves(x, half) * rope_ref[table + 1]

    scale = HEAD_DIM ** -0.5 * LOG2E
    scale_df = DF_QK_DIM ** -0.5 * LOG2E
    plain = lambda x: x
    groups = [
        (NA_Q, 4, lambda x: x * scale),
        (NA_K, 4, plain),
        (NA_V, 4, plain),
        (GQ_Q, 4, lambda x: rot(_rms(x) * qkg_ref[0:1, :], 0, 32) * scale),
        (GQ_K, 2, lambda x: rot(_rms(x) * qkg_ref[1:2, :], 0, 32)),
        (GQ_V, 2, plain),
        (DF_Q, 4, lambda x: rot(x, 2, 16) * scale_df),
        (DF_K, 4, lambda x: rot(x, 2, 16)),
        (DF_V, 4, plain),
        (SW_Q, 4, lambda x: rot(x, 0, 32) * scale),
        (SW_K, 2, lambda x: rot(x, 0, 32)),
        (SW_V, 2, plain),
    ]
    for first, count, fn in groups:
        c0 = first * HEAD_DIM
        acc = _dot(a_scr[...], w_ref[:, c0:c0 + count * HEAD_DIM])
        for j in range(count):
            x = fn(acc[:, j * HEAD_DIM:(j + 1) * HEAD_DIM])
            o_ref[:, c0 + j * HEAD_DIM:c0 + (j + 1) * HEAD_DIM] = x.astype(BF16)


def _in_projection(h, mod, mod_row, g, qkg, w_in, layer, rope_tab):
    rows = h.shape[0]
    tm = TM_PROJ
    rope = rope_tab is not None
    in_specs = [
        pl.BlockSpec((tm, D_MODEL), lambda i: (i, 0)),
        pl.BlockSpec((1, 6, D_MODEL), lambda i: (mod_row(i), 0, 0)),
        pl.BlockSpec((4, D_MODEL), lambda i: (0, 0)),
        pl.BlockSpec((2, HEAD_DIM), lambda i: (0, 0)),
        _resident_layer(layer, (D_MODEL, IN_WIDTH)),
    ]
    args = [h, mod, g, qkg, w_in]
    if rope:
        tiles_per_seq = SEQ // tm
        in_specs.append(pl.BlockSpec((4, tm, HEAD_DIM), lambda i: (0, i % tiles_per_seq, 0)))
        args.append(rope_tab)
    return pl.pallas_call(
        functools.partial(_inproj_kernel, rope=rope),
        out_shape=jax.ShapeDtypeStruct((rows, IN_WIDTH), BF16),
        grid=(rows // tm,),
        in_specs=in_specs,
        out_specs=pl.BlockSpec((tm, IN_WIDTH), lambda i: (i, 0)),
        scratch_shapes=[pltpu.VMEM((tm, D_MODEL), BF16)],
        compiler_params=_params(56),
        name="in_projection",
    )(*args)


def _softmax_pv(s, v, extra_logit=None):
    m = jnp.max(s, axis=-1, keepdims=True)
    if extra_logit is not None:
        m = jnp.maximum(m, extra_logit)
    p = jnp.exp2(s - m)
    l = jnp.sum(p, axis=-1, keepdims=True)
    if extra_logit is not None:
        l = l + jnp.exp2(extra_logit - m)
    return _dot(p.astype(BF16), v) * (1.0 / l)


def _softmax_pv_blocks(scores, values):
    m = functools.reduce(jnp.maximum, [jnp.max(s, axis=-1, keepdims=True) for s in scores])
    l = None
    o = None
    for s, v in zip(scores, values):
        p = jnp.exp2(s - m)
        li = jnp.sum(p, axis=-1, keepdims=True)
        oi = _dot(p.astype(BF16), v)
        l = li if l is None else l + li
        o = oi if o is None else o + oi
    return o * (1.0 / l)


def _rows(*blocks):
    return jnp.concatenate(blocks, axis=0)


def _online_softmax_pv(q, key_chunks, value_chunks):
    n = len(key_chunks)
    s_next = _dot_nt(q, key_chunks[0]())
    m = None
    acc = None
    for i in range(n):
        s = s_next
        if i + 1 < n:
            s_next = _dot_nt(q, key_chunks[i + 1]())
        mi = jnp.max(s, axis=-1, keepdims=True)
        m_new = mi if m is None else jnp.maximum(m, mi)
        pv = _dot(jnp.exp2(s - m_new).astype(BF16), value_chunks[i]())
        acc = pv if acc is None else acc * jnp.exp2(m - m_new) + pv
        m = m_new
    return acc


def _fill_value_ones(vaug, vc_ref, vl_ref):
    vaug[0:CTX_LEN, 0:HEAD_DIM] = vc_ref[...]
    vaug[CTX_LEN:, 0:HEAD_DIM] = vl_ref[...]
    vaug[:, HEAD_DIM:] = jnp.ones((CTX_LEN + SEQ, HEAD_DIM), BF16)


def _flash_chunks(kl_ref, kc_ref, vaug):
    keys, values = [], []
    start = 0
    for size in FLASH_CHUNKS:
        keys.append(lambda a=start, b=start + size: kl_ref[a:b, :])
        values.append(lambda a=start, b=start + size: vaug[CTX_LEN + a:CTX_LEN + b, :])
        start += size
    keys.append(lambda: kc_ref[...])
    values.append(lambda: vaug[0:CTX_LEN, :])
    return keys, values


def _diff_lambda(lam_ref, lam_init):
    lam = lam_ref[...]
    a = jnp.sum(lam[0:1, :] * lam[1:2, :], axis=-1, keepdims=True)
    b = jnp.sum(lam[2:3, :] * lam[3:4, :], axis=-1, keepdims=True)
    return jnp.exp(a) - jnp.exp(b) + lam_init


def _split_diff_query(q):
    lane = lax.broadcasted_iota(jnp.int32, q.shape, 1)
    zero = jnp.zeros_like(q)
    return jnp.concatenate([jnp.where(lane < DF_QK_DIM, q, zero),
                            jnp.where(lane >= DF_QK_DIM, q, zero)], axis=0)


def _na_kernel(q_ref, kl_ref, vl_ref, kc_ref, vc_ref, b0_ref, b1_ref, b2_ref, o_ref):
    step = pl.program_id(2)
    bias_refs = [b0_ref] + [b1_ref] * (NA_SUB - 2) + [b2_ref]

    def scores(j):
        t = NA_SUB * step + j
        first_row = jnp.clip(NA_QROWS * t - NA_KH // 2, 0, GRID_ROWS - NA_KROWS)
        start = pl.multiple_of(first_row * GRID_W, TQ_NA)
        q = q_ref[j * TQ_NA:(j + 1) * TQ_NA, :]
        s_loc = _dot_nt(q, kl_ref[pl.ds(start, TK_NA), :]) + bias_refs[j][0, 0]
        return start, [s_loc, _dot_nt(q, kc_ref[...])]

    nxt = scores(0)
    for j in range(NA_SUB):
        start, s = nxt
        if j + 1 < NA_SUB:
            nxt = scores(j + 1)
        o = _softmax_pv_blocks(s, [vl_ref[pl.ds(start, TK_NA), :], vc_ref[...]])
        o_ref[j * TQ_NA:(j + 1) * TQ_NA, :] = o.astype(BF16)


def _na_window_geometry(t):
    first_row = min(max(NA_QROWS * t - NA_KH // 2, 0), GRID_ROWS - NA_KROWS)
    r = NA_QROWS * t + np.arange(NA_QROWS)
    kr = first_row + np.arange(NA_KROWS)
    c = np.arange(GRID_W)
    kc = np.arange(GRID_W)
    r0 = np.clip(r - NA_KH // 2, 0, GRID_ROWS - NA_KH)
    c0 = np.clip(c - NA_KW // 2, 0, GRID_W - NA_KW)
    ok_r = (kr[None, :] >= r0[:, None]) & (kr[None, :] < r0[:, None] + NA_KH)
    ok_c = (kc[None, :] >= c0[:, None]) & (kc[None, :] < c0[:, None] + NA_KW)
    dr = np.clip(kr[None, :] - r[:, None] + NA_KH - 1, 0, 2 * NA_KH - 2)
    dc = np.clip(kc[None, :] - c[:, None] + NA_KW - 1, 0, 2 * NA_KW - 2)
    return dr, dc, ok_r, ok_c


def _na_bias_table(na_rpb):
    rpb = na_rpb.astype(F32) * LOG2E
    geometry = [_na_window_geometry(t) for t in (0, 1, GRID_ROWS // NA_QROWS - 1)]
    dc, ok_c = geometry[0][1], geometry[0][3]
    dr = np.stack([g[0] for g in geometry])
    ok_r = np.stack([g[2] for g in geometry])
    pick_r = (dr[..., None] == np.arange(2 * NA_KH - 1)).astype(np.float32)
    pick_c = (dc[..., None] == np.arange(2 * NA_KW - 1)).astype(np.float32)
    rows = jnp.einsum("lhde,vrid->lvhrie", rpb, pick_r, precision=lax.Precision.HIGHEST)
    vals = jnp.einsum("lvhrie,cke->lvhrcik", rows, pick_c, precision=lax.Precision.HIGHEST)
    ok = ok_r[:, None, :, None, :, None] & ok_c[None, None, None, :, None, :]
    return jnp.where(ok[None], vals, NEG_INF).reshape(DEPTH, 3, 4, TQ_NA, TK_NA)


def _na_attention(qkv, qkv_c, bias, layer):
    tq = NA_SUB * TQ_NA
    nt = SEQ // tq
    kv = lambda col: pl.BlockSpec((SEQ, HEAD_DIM), lambda b, h, t: (b, col + h))
    kvc = lambda col: pl.BlockSpec((CTX_LEN, HEAD_DIM), lambda b, h, t: (b, col + h))
    bias_block = (None, 1, 1, TQ_NA, TK_NA)
    return pl.pallas_call(
        _na_kernel,
        out_shape=jax.ShapeDtypeStruct((BATCH * SEQ, 4 * HEAD_DIM), BF16),
        grid=(BATCH, 4, nt),
        in_specs=[
            pl.BlockSpec((tq, HEAD_DIM), lambda b, h, t: (b * nt + t, NA_Q + h)),
            kv(NA_K), kv(NA_V), kvc(NA_K), kvc(NA_V),
            pl.BlockSpec(bias_block, lambda b, h, t: (layer, jnp.where(t == 0, 0, 1), h, 0, 0)),
            pl.BlockSpec(bias_block, lambda b, h, t: (layer, 1, h, 0, 0)),
            pl.BlockSpec(bias_block, lambda b, h, t: (layer, jnp.where(t == nt - 1, 2, 1), h, 0, 0)),
        ],
        out_specs=pl.BlockSpec((tq, HEAD_DIM), lambda b, h, t: (b * nt + t, h)),
        compiler_params=_params(40),
        name="na_attention",
    )(qkv, qkv, qkv, qkv_c, qkv_c, bias, bias, bias)


def _gq_kernel(q_ref, kl_ref, vl_ref, kc_ref, vc_ref, o_ref, vaug):
    @pl.when(pl.program_id(2) == 0)
    def _():
        _fill_value_ones(vaug, vc_ref, vl_ref)

    q = jnp.concatenate([q_ref[:, 0:HEAD_DIM], q_ref[:, HEAD_DIM:]], axis=0)
    acc = _online_softmax_pv(q, *_flash_chunks(kl_ref, kc_ref, vaug))
    o = acc[:, 0:HEAD_DIM] / acc[:, HEAD_DIM:]
    o_ref[:, 0:HEAD_DIM] = o[0:TQ_FLASH].astype(BF16)
    o_ref[:, HEAD_DIM:] = o[TQ_FLASH:].astype(BF16)


def _gq_attention(qkv, qkv_c):
    nt = SEQ // TQ_FLASH
    kv = lambda col: pl.BlockSpec((SEQ, HEAD_DIM), lambda b, k, t: (b, col + k))
    kvc = lambda col: pl.BlockSpec((CTX_LEN, HEAD_DIM), lambda b, k, t: (b, col + k))
    return pl.pallas_call(
        _gq_kernel,
        out_shape=jax.ShapeDtypeStruct((BATCH * SEQ, 4 * HEAD_DIM), BF16),
        grid=(BATCH, 2, nt),
        in_specs=[
            pl.BlockSpec((TQ_FLASH, 2 * HEAD_DIM), lambda b, k, t: (b * nt + t, GQ_Q // 2 + k)),
            kv(GQ_K), kv(GQ_V), kvc(GQ_K), kvc(GQ_V),
        ],
        out_specs=pl.BlockSpec((TQ_FLASH, 2 * HEAD_DIM), lambda b, k, t: (b * nt + t, k)),
        scratch_shapes=[pltpu.VMEM((CTX_LEN + SEQ, 2 * HEAD_DIM), BF16)],
        compiler_params=_params(48),
        name="gq_attention",
    )(qkv, qkv, qkv, qkv_c, qkv_c)


def _df_kernel(lam_ref, subg_ref, q_ref, kl_ref, vl_ref, kc_ref, vc_ref, o_ref, vaug, *, lam_init):
    @pl.when(pl.program_id(2) == 0)
    def _():
        _fill_value_ones(vaug, vc_ref, vl_ref)

    lam = _diff_lambda(lam_ref, lam_init)
    acc = _online_softmax_pv(_split_diff_query(q_ref[...]), *_flash_chunks(kl_ref, kc_ref, vaug))
    o = acc[:, 0:HEAD_DIM] / acc[:, HEAD_DIM:]
    o = o[0:TQ_FLASH] - lam * o[TQ_FLASH:]
    o_ref[...] = (_rms(o) * (subg_ref[...] * (1.0 - lam_init))).astype(BF16)


def _df_attention(qkv, qkv_c, df_lambda, subg, lam_init):
    nt = SEQ // TQ_FLASH
    kv = lambda col: pl.BlockSpec((SEQ, HEAD_DIM), lambda b, h, t: (b, col + h))
    kvc = lambda col: pl.BlockSpec((CTX_LEN, HEAD_DIM), lambda b, h, t: (b, col + h))
    return pl.pallas_call(
        functools.partial(_df_kernel, lam_init=lam_init),
        out_shape=jax.ShapeDtypeStruct((BATCH * SEQ, 4 * HEAD_DIM), BF16),
        grid=(BATCH, 4, nt),
        in_specs=[
            pl.BlockSpec((4, DF_QK_DIM), lambda b, h, t: (0, 0)),
            pl.BlockSpec((1, HEAD_DIM), lambda b, h, t: (0, 0)),
            pl.BlockSpec((TQ_FLASH, HEAD_DIM), lambda b, h, t: (b * nt + t, DF_Q + h)),
            kv(DF_K), kv(DF_V), kvc(DF_K), kvc(DF_V),
        ],
        out_specs=pl.BlockSpec((TQ_FLASH, HEAD_DIM), lambda b, h, t: (b * nt + t, h)),
        scratch_shapes=[pltpu.VMEM((CTX_LEN + SEQ, 2 * HEAD_DIM), BF16)],
        compiler_params=_params(48),
        name="df_attention",
    )(df_lambda, subg, qkv, qkv, qkv, qkv_c, qkv_c)


def _sw_kernel(sink_ref, q_ref, kl_ref, vl_ref, kc_ref, vc_ref, m0_ref, m1_ref, m2_ref, o_ref):
    step = pl.program_id(2)
    mask_refs = [m0_ref] + [m1_ref] * (SW_SUB - 2) + [m2_ref]
    head = lax.broadcasted_iota(jnp.int32, (2 * TQ_SW, 1), 0) >= TQ_SW
    sink = jnp.where(head, sink_ref[0, 1:2, 0:1], sink_ref[0, 0:1, 0:1]) * LOG2E

    for j in range(SW_SUB):
        t = SW_SUB * step + j
        start = pl.multiple_of(jnp.clip(TQ_SW * t - SW_WINDOW, 0, SEQ - TK_SW), SW_WINDOW)
        keys = _rows(kl_ref[pl.ds(start, TK_SW), :], kc_ref[...])
        values = _rows(vl_ref[pl.ds(start, TK_SW), :], vc_ref[...])
        rows = slice(j * TQ_SW, (j + 1) * TQ_SW)
        q = _rows(q_ref[rows, 0:HEAD_DIM], q_ref[rows, HEAD_DIM:])
        o = _softmax_pv(_dot_nt(q, keys) + mask_refs[j][0], values, extra_logit=sink)
        o_ref[rows, 0:HEAD_DIM] = o[0:TQ_SW].astype(BF16)
        o_ref[rows, HEAD_DIM:] = o[TQ_SW:].astype(BF16)


def _sw_mask_table():
    r = np.arange(TQ_SW)[:, None]
    c = np.arange(TK_SW)[None, :]
    nt = SEQ // TQ_SW
    tables = []
    for t in (0, 1, nt - 1):
        start = min(max(TQ_SW * t - SW_WINDOW, 0), SEQ - TK_SW)
        ok = np.abs((TQ_SW * t + r) - (start + c)) <= SW_WINDOW
        band = np.where(ok, 0.0, NEG_INF).astype(np.float32)
        band = np.concatenate([band, np.zeros((TQ_SW, CTX_LEN), np.float32)], axis=1)
        tables.append(np.concatenate([band, band], axis=0))
    return jnp.asarray(np.stack(tables))


def _sw_attention(qkv, qkv_c, sink):
    tq = SW_SUB * TQ_SW
    nt = SEQ // tq
    mask = _sw_mask_table()
    kv = lambda col: pl.BlockSpec((SEQ, HEAD_DIM), lambda b, k, t: (b, col + k))
    kvc = lambda col: pl.BlockSpec((CTX_LEN, HEAD_DIM), lambda b, k, t: (b, col + k))
    mask_block = (1, 2 * TQ_SW, TK_SW + CTX_LEN)
    return pl.pallas_call(
        _sw_kernel,
        out_shape=jax.ShapeDtypeStruct((BATCH * SEQ, 4 * HEAD_DIM), BF16),
        grid=(BATCH, 2, nt),
        in_specs=[
            pl.BlockSpec((1, 2, HEAD_DIM), lambda b, k, t: (k, 0, 0)),
            pl.BlockSpec((tq, 2 * HEAD_DIM), lambda b, k, t: (b * nt + t, SW_Q // 2 + k)),
            kv(SW_K), kv(SW_V), kvc(SW_K), kvc(SW_V),
            pl.BlockSpec(mask_block, lambda b, k, t: (jnp.where(t == 0, 0, 1), 0, 0)),
            pl.BlockSpec(mask_block, lambda b, k, t: (1, 0, 0)),
            pl.BlockSpec(mask_block, lambda b, k, t: (jnp.where(t == nt - 1, 2, 1), 0, 0)),
        ],
        out_specs=pl.BlockSpec((tq, 2 * HEAD_DIM), lambda b, k, t: (b * nt + t, k)),
        compiler_params=_params(40),
        name="sw_attention",
    )(sink.reshape(2, 2, HEAD_DIM), qkv, qkv, qkv, qkv_c, qkv_c, mask, mask, mask)


def _ctx_attn_kernel(lam_ref, subg_ref, sink_ref, x_ref, o_ref, *, lam_init):
    chunk = lambda j: x_ref[:, j * HEAD_DIM:(j + 1) * HEAD_DIM]

    def put(j, o):
        o_ref[:, j * HEAD_DIM:(j + 1) * HEAD_DIM] = o.astype(BF16)

    lam = _diff_lambda(lam_ref, lam_init)
    for h in range(4):
        put(h, _softmax_pv(_dot_nt(chunk(NA_Q + h), chunk(NA_K + h)), chunk(NA_V + h)))
        put(4 + h, _softmax_pv(_dot_nt(chunk(GQ_Q + h), chunk(GQ_K + h // 2)),
                               chunk(GQ_V + h // 2)))
        q2 = _split_diff_query(chunk(DF_Q + h))
        o = _softmax_pv(_dot_nt(q2, chunk(DF_K + h)), chunk(DF_V + h))
        o = o[0:CTX_LEN] - lam * o[CTX_LEN:]
        put(8 + h, _rms(o) * (subg_ref[...] * (1.0 - lam_init)))
        put(12 + h, _softmax_pv(_dot_nt(chunk(SW_Q + h), chunk(SW_K + h // 2)),
                                chunk(SW_V + h // 2),
                                extra_logit=sink_ref[h:h + 1, 0:1] * LOG2E))


def _ctx_attention(qkv_c, df_lambda, subg, sink, lam_init):
    return pl.pallas_call(
        functools.partial(_ctx_attn_kernel, lam_init=lam_init),
        out_shape=jax.ShapeDtypeStruct((BATCH * CTX_LEN, D_MODEL), BF16),
        grid=(BATCH,),
        in_specs=[
            pl.BlockSpec((4, DF_QK_DIM), lambda b: (0, 0)),
            pl.BlockSpec((1, HEAD_DIM), lambda b: (0, 0)),
            pl.BlockSpec((4, HEAD_DIM), lambda b: (0, 0)),
            pl.BlockSpec((CTX_LEN, IN_WIDTH), lambda b: (b, 0)),
        ],
        out_specs=pl.BlockSpec((CTX_LEN, D_MODEL), lambda b: (b, 0)),
        compiler_params=_params(32),
        name="ctx_attention",
    )(df_lambda, subg, sink, qkv_c)


def _outproj_kernel(y0_ref, y1_ref, y2_ref, y3_ref, h_ref, mod_ref, g_ref, w_ref, o_ref, f_ref):
    gate = mod_ref[0, 2:3, :] * g_ref[1:2, :]
    ffn_gain = _gain(g_ref[2:3, :], mod_ref[0, 4:5, :])
    for c in range(TM_PROJ // ROW_SUB):
        rows = slice(c * ROW_SUB, (c + 1) * ROW_SUB)
        y = jnp.concatenate([y_ref[rows, :] for y_ref in (y0_ref, y1_ref, y2_ref, y3_ref)], axis=1)
        h_new = h_ref[rows, :] + _rms(_dot(y, w_ref[...])) * gate
        o_ref[rows, :] = h_new
        f_ref[rows, :] = (_rms(h_new) * ffn_gain + mod_ref[0, 3:4, :]).astype(BF16)


def _out_projection(ys, y_cols, h, mod, mod_row, g, w_out, layer):
    rows = h.shape[0]
    tm = TM_PROJ
    y_specs = [pl.BlockSpec((tm, 512), lambda i, c=c: (i, c)) for c in y_cols]
    row_block = pl.BlockSpec((tm, D_MODEL), lambda i: (i, 0))
    return pl.pallas_call(
        _outproj_kernel,
        out_shape=(jax.ShapeDtypeStruct((rows, D_MODEL), F32),
                   jax.ShapeDtypeStruct((rows, D_MODEL), BF16)),
        grid=(rows // tm,),
        in_specs=y_specs + [
            row_block,
            pl.BlockSpec((1, 6, D_MODEL), lambda i: (mod_row(i), 0, 0)),
            pl.BlockSpec((4, D_MODEL), lambda i: (0, 0)),
            _resident_layer(layer, (D_MODEL, D_MODEL)),
        ],
        out_specs=(row_block, row_block),
        compiler_params=_params(48),
        name="out_projection",
    )(*ys, h, mod, g, w_out)


def _ffn_up_kernel(f_ref, wg_ref, wu_ref, o_ref, wg_scr, wu_scr):
    @pl.when(pl.program_id(1) == 0)
    def _():
        wg_scr[...] = wg_ref[...].astype(BF16)
        wu_scr[...] = wu_ref[...].astype(BF16)

    f = f_ref[...]
    gate = _dot(f, wg_scr[...])
    up = _dot(f, wu_scr[...])
    o_ref[...] = (gate * jax.nn.sigmoid(gate) * up).astype(BF16)


def _ffn_up(f, w_gate_up, layer):
    rows = f.shape[0]
    tm, tf = min(TM_UP, rows), TF_UP
    nf = D_FF // tf
    return pl.pallas_call(
        _ffn_up_kernel,
        out_shape=jax.ShapeDtypeStruct((rows, D_FF), BF16),
        grid=(nf, rows // tm),
        in_specs=[
            pl.BlockSpec((tm, D_MODEL), lambda k, i: (i, 0)),
            pl.BlockSpec((None, D_MODEL, tf), lambda k, i: (layer, 0, k)),
            pl.BlockSpec((None, D_MODEL, tf), lambda k, i: (layer, 0, nf + k)),
        ],
        out_specs=pl.BlockSpec((tm, tf), lambda k, i: (i, k)),
        scratch_shapes=[pltpu.VMEM((D_MODEL, tf), BF16), pltpu.VMEM((D_MODEL, tf), BF16)],
        compiler_params=_params(48),
        name="ffn_up",
    )(f, w_gate_up, w_gate_up)


def _ffn_down_kernel(a_ref, h_ref, mod_ref, g_ref, w_ref, o_ref):
    gate = mod_ref[0, 5:6, :] * g_ref[3:4, :]
    for c in range(TM_PROJ // ROW_SUB):
        rows = slice(c * ROW_SUB, (c + 1) * ROW_SUB)
        acc = _dot(a_ref[rows, :], w_ref[...])
        o_ref[rows, :] = h_ref[rows, :] + _rms(acc) * gate


def _ffn_down(act, h, mod, mod_row, g, w_down, layer):
    rows = h.shape[0]
    tm = TM_PROJ
    row_block = pl.BlockSpec((tm, D_MODEL), lambda i: (i, 0))
    return pl.pallas_call(
        _ffn_down_kernel,
        out_shape=jax.ShapeDtypeStruct((rows, D_MODEL), F32),
        grid=(rows // tm,),
        in_specs=[
            pl.BlockSpec((tm, D_FF), lambda i: (i, 0)),
            row_block,
            pl.BlockSpec((1, 6, D_MODEL), lambda i: (mod_row(i), 0, 0)),
            pl.BlockSpec((4, D_MODEL), lambda i: (0, 0)),
            _resident_layer(layer, (D_FF, D_MODEL)),
        ],
        out_specs=row_block,
        compiler_params=_params(58),
        name="ffn_down",
    )(act, h, mod, g, w_down)


def _rope_table(dim):
    q = dim // 4
    t = jnp.arange(SEQ)
    pos = jnp.stack([t // GRID_W, t % GRID_W], axis=-1).astype(F32)
    inv = ROPE_THETA ** (-jnp.arange(q, dtype=F32) / q)
    ang = pos[:, :, None] * inv
    cos = jnp.broadcast_to(jnp.cos(ang)[:, :, None, :], (SEQ, 2, 2, q)).reshape(SEQ, dim)
    sin = jnp.sin(ang)
    sin = jnp.stack([-sin, sin], axis=2).reshape(SEQ, dim)
    reps = HEAD_DIM // dim
    return jnp.tile(cos, (1, reps)), jnp.tile(sin, (1, reps))


def kernel(x, c, ctx, c_ctx, w_mod, b_mod, norm_g, w_in, w_out, na_rpb, qk_norm_g,
           df_lambda, df_subln_g, sw_sink, w_gate_up, w_down):
    cvec = jnp.concatenate([c_ctx[None, :], c, jnp.zeros((3, D_MODEL), F32)], axis=0)
    mod_all = _modulation(cvec, w_mod, b_mod).reshape(DEPTH, 8, 6, D_MODEL)
    rope_tab = jnp.stack(_rope_table(HEAD_DIM) + _rope_table(DF_QK_DIM))
    na_bias = _na_bias_table(na_rpb)

    tiles_per_seq = SEQ // TM_PROJ
    lat_row = lambda i: 1 + i // tiles_per_seq
    ctx_row = lambda i: 0

    w_in, w_out, w_down = (w.astype(BF16) for w in (w_in, w_out, w_down))

    h = x.reshape(BATCH * SEQ, D_MODEL)
    hc = ctx.reshape(BATCH * CTX_LEN, D_MODEL)
    for l in range(DEPTH):
        ctx_out = l < DEPTH - 1
        lam_init = 0.8 - 0.6 * math.exp(-0.3 * l)
        mod = mod_all[l]
        g = norm_g[l]
        subg = df_subln_g[l][None, :]
        sink = jnp.broadcast_to(sw_sink[l][:, None], (4, HEAD_DIM))

        qkv = _in_projection(h, mod, lat_row, g, qk_norm_g[l], w_in, l, rope_tab)
        qkv_c = _in_projection(hc, mod, ctx_row, g, qk_norm_g[l], w_in, l, None)

        ys = [
            _na_attention(qkv, qkv_c, na_bias, l),
            _gq_attention(qkv, qkv_c),
            _df_attention(qkv, qkv_c, df_lambda[l], subg, lam_init),
            _sw_attention(qkv, qkv_c, sink),
        ]
        h, f = _out_projection(ys, (0, 0, 0, 0), h, mod, lat_row, g, w_out, l)
        h = _ffn_down(_ffn_up(f, w_gate_up, l), h, mod, lat_row, g, w_down, l)
        if ctx_out:
            yc = _ctx_attention(qkv_c, df_lambda[l], subg, sink, lam_init)
            hc, fc = _out_projection([yc] * 4, (0, 1, 2, 3), hc, mod, ctx_row, g, w_out, l)
            hc = _ffn_down(_ffn_up(fc, w_gate_up, l), hc, mod, ctx_row, g, w_down, l)
    return h.reshape(BATCH, SEQ, D_MODEL)
```

```python
import functools
import math

import jax
import jax.numpy as jnp
import numpy as np
from jax import lax
from jax.experimental import pallas as pl
from jax.experimental.pallas import tpu as pltpu

D_MODEL = 2048
BATCH = 4
SEQ = 4096
DEPTH = 2
GRID_W = 64
GRID_ROWS = SEQ // GRID_W
CTX_LEN = 256
HEAD_DIM = 128
NA_KH = 8
NA_KW = 16
DF_QK_DIM = 64
SW_WINDOW = 128
D_FF = 5632
IN_WIDTH = 5120
ROPE_THETA = 10000.0
EPS = 1e-6
NEG_INF = -1e30
LOG2E = math.log2(math.e)

NA_Q, NA_K, NA_V = 0, 4, 8
GQ_Q, GQ_K, GQ_V = 12, 16, 18
DF_Q, DF_K, DF_V = 20, 24, 28
SW_Q, SW_K, SW_V = 32, 36, 38

F32 = jnp.float32
BF16 = jnp.bfloat16
MIB = 1024 * 1024

TM_PROJ = 512
ROW_SUB = 256
TM_UP = 1024
TF_UP = 512
TN_MOD = 1536
TQ_FLASH = 512
FLASH_CHUNKS = (2048, 2048)
assert sum(FLASH_CHUNKS) == SEQ
NA_QROWS = 4
NA_KROWS = 12
TQ_NA = NA_QROWS * GRID_W
TK_NA = NA_KROWS * GRID_W
TQ_SW = 256
TK_SW = TQ_SW + 2 * SW_WINDOW
NA_SUB = 8
SW_SUB = 4


def _params(vmem_mib):
    return pltpu.CompilerParams(vmem_limit_bytes=vmem_mib * MIB)


def _resident_layer(layer, shape):
    index = (layer,) + (0,) * len(shape)
    return pl.BlockSpec((None,) + tuple(shape), lambda *_: index, pipeline_mode=pl.Buffered(1))


def _gain(g, scale=None):
    return g if scale is None else g * (1.0 + scale)


def _rms(x):
    return x * lax.rsqrt(jnp.mean(x * x, axis=-1, keepdims=True) + EPS)


def _dot(a, b):
    return jnp.dot(a, b, preferred_element_type=F32)


def _dot_nt(a, b):
    return lax.dot_general(a, b, (((1,), (1,)), ((), ())), preferred_element_type=F32)


def _mod_kernel(c_ref, w_ref, b_ref, o_ref):
    c = c_ref[...]
    s = c * jax.nn.sigmoid(c)
    hi = s.astype(BF16)
    lo = (s - hi.astype(F32)).astype(BF16)
    r = _dot(jnp.concatenate([hi, lo], axis=0), w_ref[0].astype(BF16))
    o_ref[0] = r[0:8] + r[8:16] + b_ref[0]


def _modulation(cvec, w_mod, b_mod):
    n = 6 * D_MODEL
    return pl.pallas_call(
        _mod_kernel,
        out_shape=jax.ShapeDtypeStruct((DEPTH, 8, n), F32),
        grid=(DEPTH, n // TN_MOD),
        in_specs=[
            pl.BlockSpec((8, D_MODEL), lambda l, j: (0, 0)),
            pl.BlockSpec((1, D_MODEL, TN_MOD), lambda l, j: (l, 0, j)),
            pl.BlockSpec((1, 1, TN_MOD), lambda l, j: (l, 0, j)),
        ],
        out_specs=pl.BlockSpec((1, 8, TN_MOD), lambda l, j: (l, 0, j)),
        compiler_params=_params(40),
        name="modulation",
    )(cvec, w_mod, b_mod.reshape(DEPTH, 1, n))


def _swap_halves(x, half):
    lane = lax.broadcasted_iota(jnp.int32, x.shape, 1)
    fwd = pltpu.roll(x, HEAD_DIM - half, axis=1)
    bwd = pltpu.roll(x, half, axis=1)
    return jnp.where((lane & (2 * half - 1)) < half, fwd, bwd)


def _inproj_kernel(*refs, rope):
    if rope:
        h_ref, mod_ref, g_ref, qkg_ref, w_ref, rope_ref, o_ref, a_scr = refs
    else:
        h_ref, mod_ref, g_ref, qkg_ref, w_ref, o_ref, a_scr = refs
        rope_ref = None
    a = _rms(h_ref[...]) * _gain(g_ref[0:1, :], mod_ref[0, 1:2, :]) + mod_ref[0, 0:1, :]
    a_scr[...] = a.astype(BF16)

    def rot(x, table, half):
        if rope_ref is None:
            return x
        return x * rope_ref[table] + _swap_halves(x, half) * rope_ref[table + 1]

    scale = HEAD_DIM ** -0.5 * LOG2E
    scale_df = DF_QK_DIM ** -0.5 * LOG2E
    plain = lambda x: x
    groups = [
        (NA_Q, 4, lambda x: x * scale),
        (NA_K, 4, plain),
        (NA_V, 4, plain),
        (GQ_Q, 4, lambda x: rot(_rms(x) * qkg_ref[0:1, :], 0, 32) * scale),
        (GQ_K, 2, lambda x: rot(_rms(x) * qkg_ref[1:2, :], 0, 32)),
        (GQ_V, 2, plain),
        (DF_Q, 4, lambda x: rot(x, 2, 16) * scale_df),
        (DF_K, 4, lambda x: rot(x, 2, 16)),
        (DF_V, 4, plain),
        (SW_Q, 4, lambda x: rot(x, 0, 32) * scale),
        (SW_K, 2, lambda x: rot(x, 0, 32)),
        (SW_V, 2, plain),
    ]
    for first, count, fn in groups:
        c0 = first * HEAD_DIM
        acc = _dot(a_scr[...], w_ref[:, c0:c0 + count * HEAD_DIM])
        for j in range(count):
            x = fn(acc[:, j * HEAD_DIM:(j + 1) * HEAD_DIM])
            o_ref[:, c0 + j * HEAD_DIM:c0 + (j + 1) * HEAD_DIM] = x.astype(BF16)


def _in_projection(h, mod, mod_row, g, qkg, w_in, layer, rope_tab):
    rows = h.shape[0]
    tm = TM_PROJ
    rope = rope_tab is not None
    in_specs = [
        pl.BlockSpec((tm, D_MODEL), lambda i: (i, 0)),
        pl.BlockSpec((1, 6, D_MODEL), lambda i: (mod_row(i), 0, 0)),
        pl.BlockSpec((4, D_MODEL), lambda i: (0, 0)),
        pl.BlockSpec((2, HEAD_DIM), lambda i: (0, 0)),
        _resident_layer(layer, (D_MODEL, IN_WIDTH)),
    ]
    args = [h, mod, g, qkg, w_in]
    if rope:
        tiles_per_seq = SEQ // tm
        in_specs.append(pl.BlockSpec((4, tm, HEAD_DIM), lambda i: (0, i % tiles_per_seq, 0)))
        args.append(rope_tab)
    return pl.pallas_call(
        functools.partial(_inproj_kernel, rope=rope),
        out_shape=jax.ShapeDtypeStruct((rows, IN_WIDTH), BF16),
        grid=(rows // tm,),
        in_specs=in_specs,
        out_specs=pl.BlockSpec((tm, IN_WIDTH), lambda i: (i, 0)),
        scratch_shapes=[pltpu.VMEM((tm, D_MODEL), BF16)],
        compiler_params=_params(56),
        name="in_projection",
    )(*args)


def _softmax_pv(s, v, extra_logit=None):
    m = jnp.max(s, axis=-1, keepdims=True)
    if extra_logit is not None:
        m = jnp.maximum(m, extra_logit)
    p = jnp.exp2(s - m)
    l = jnp.sum(p, axis=-1, keepdims=True)
    if extra_logit is not None:
        l = l + jnp.exp2(extra_logit - m)
    return _dot(p.astype(BF16), v) * (1.0 / l)


def _softmax_pv_blocks(scores, values):
    m = functools.reduce(jnp.maximum, [jnp.max(s, axis=-1, keepdims=True) for s in scores])
    l = None
    o = None
    for s, v in zip(scores, values):
        p = jnp.exp2(s - m)
        li = jnp.sum(p, axis=-1, keepdims=True)
        oi = _dot(p.astype(BF16), v)
        l = li if l is None else l + li
        o = oi if o is None else o + oi
    return o * (1.0 / l)


def _rows(*blocks):
    return jnp.concatenate(blocks, axis=0)


def _online_softmax_pv(q, key_chunks, value_chunks):
    n = len(key_chunks)
    s_next = _dot_nt(q, key_chunks[0]())
    m = None
    acc = None
    for i in range(n):
        s = s_next
        if i + 1 < n:
            s_next = _dot_nt(q, key_chunks[i + 1]())
        mi = jnp.max(s, axis=-1, keepdims=True)
        m_new = mi if m is None else jnp.maximum(m, mi)
        pv = _dot(jnp.exp2(s - m_new).astype(BF16), value_chunks[i]())
        acc = pv if acc is None else acc * jnp.exp2(m - m_new) + pv
        m = m_new
    return acc


def _fill_value_ones(vaug, vc_ref, vl_ref):
    vaug[0:CTX_LEN, 0:HEAD_DIM] = vc_ref[...]
    vaug[CTX_LEN:, 0:HEAD_DIM] = vl_ref[...]
    vaug[:, HEAD_DIM:] = jnp.ones((CTX_LEN + SEQ, HEAD_DIM), BF16)


def _flash_chunks(kl_ref, kc_ref, vaug):
    keys, values = [], []
    start = 0
    for size in FLASH_CHUNKS:
        keys.append(lambda a=start, b=start + size: kl_ref[a:b, :])
        values.append(lambda a=start, b=start + size: vaug[CTX_LEN + a:CTX_LEN + b, :])
        start += size
    keys.append(lambda: kc_ref[...])
    values.append(lambda: vaug[0:CTX_LEN, :])
    return keys, values


def _diff_lambda(lam_ref, lam_init):
    lam = lam_ref[...]
    a = jnp.sum(lam[0:1, :] * lam[1:2, :], axis=-1, keepdims=True)
    b = jnp.sum(lam[2:3, :] * lam[3:4, :], axis=-1, keepdims=True)
    return jnp.exp(a) - jnp.exp(b) + lam_init


def _split_diff_query(q):
    lane = lax.broadcasted_iota(jnp.int32, q.shape, 1)
    zero = jnp.zeros_like(q)
    return jnp.concatenate([jnp.where(lane < DF_QK_DIM, q, zero),
                            jnp.where(lane >= DF_QK_DIM, q, zero)], axis=0)


def _na_kernel(q_ref, kl_ref, vl_ref, kc_ref, vc_ref, b0_ref, b1_ref, b2_ref, o_ref):
    step = pl.program_id(2)
    bias_refs = [b0_ref] + [b1_ref] * (NA_SUB - 2) + [b2_ref]

    def scores(j):
        t = NA_SUB * step + j
        first_row = jnp.clip(NA_QROWS * t - NA_KH // 2, 0, GRID_ROWS - NA_KROWS)
        start = pl.multiple_of(first_row * GRID_W, TQ_NA)
        q = q_ref[j * TQ_NA:(j + 1) * TQ_NA, :]
        s_loc = _dot_nt(q, kl_ref[pl.ds(start, TK_NA), :]) + bias_refs[j][0, 0]
        return start, [s_loc, _dot_nt(q, kc_ref[...])]

    nxt = scores(0)
    for j in range(NA_SUB):
        start, s = nxt
        if j + 1 < NA_SUB:
            nxt = scores(j + 1)
        o = _softmax_pv_blocks(s, [vl_ref[pl.ds(start, TK_NA), :], vc_ref[...]])
        o_ref[j * TQ_NA:(j + 1) * TQ_NA, :] = o.astype(BF16)


def _na_window_geometry(t):
    first_row = min(max(NA_QROWS * t - NA_KH // 2, 0), GRID_ROWS - NA_KROWS)
    r = NA_QROWS * t + np.arange(NA_QROWS)
    kr = first_row + np.arange(NA_KROWS)
    c = np.arange(GRID_W)
    kc = np.arange(GRID_W)
    r0 = np.clip(r - NA_KH // 2, 0, GRID_ROWS - NA_KH)
    c0 = np.clip(c - NA_KW // 2, 0, GRID_W - NA_KW)
    ok_r = (kr[None, :] >= r0[:, None]) & (kr[None, :] < r0[:, None] + NA_KH)
    ok_c = (kc[None, :] >= c0[:, None]) & (kc[None, :] < c0[:, None] + NA_KW)
    dr = np.clip(kr[None, :] - r[:, None] + NA_KH - 1, 0, 2 * NA_KH - 2)
    dc = np.clip(kc[None, :] - c[:, None] + NA_KW - 1, 0, 2 * NA_KW - 2)
    return dr, dc, ok_r, ok_c


def _na_bias_table(na_rpb):
    n_dr, n_dc = 2 * NA_KH - 1, 2 * NA_KW - 1
    rpb = jnp.pad(na_rpb.astype(F32) * LOG2E, ((0, 0), (0, 0), (0, 1), (0, 1)),
                  constant_values=NEG_INF)
    geometry = [_na_window_geometry(t) for t in (0, 1, GRID_ROWS // NA_QROWS - 1)]
    dc, ok_c = geometry[0][1], geometry[0][3]
    dr = np.stack([g[0] for g in geometry])
    ok_r = np.stack([g[2] for g in geometry])
    pick_r = (np.where(ok_r, dr, n_dr)[..., None] == np.arange(n_dr + 1)).astype(np.float32)
    pick_c = (np.where(ok_c, dc, n_dc)[..., None] == np.arange(n_dc + 1)).astype(np.float32)
    rows = jnp.einsum("lhde,vrid->lvhrie", rpb, pick_r, precision=lax.Precision.HIGHEST)
    vals = jnp.einsum("lvhrie,cke->lvhrcik", rows, pick_c, precision=lax.Precision.HIGHEST)
    return vals.reshape(DEPTH, 3, 4, TQ_NA, TK_NA)


def _na_attention(qkv, qkv_c, bias, layer):
    tq = NA_SUB * TQ_NA
    nt = SEQ // tq
    kv = lambda col: pl.BlockSpec((SEQ, HEAD_DIM), lambda b, h, t: (b, col + h))
    kvc = lambda col: pl.BlockSpec((CTX_LEN, HEAD_DIM), lambda b, h, t: (b, col + h))
    bias_block = (None, 1, 1, TQ_NA, TK_NA)
    return pl.pallas_call(
        _na_kernel,
        out_shape=jax.ShapeDtypeStruct((BATCH * SEQ, 4 * HEAD_DIM), BF16),
        grid=(BATCH, 4, nt),
        in_specs=[
            pl.BlockSpec((tq, HEAD_DIM), lambda b, h, t: (b * nt + t, NA_Q + h)),
            kv(NA_K), kv(NA_V), kvc(NA_K), kvc(NA_V),
            pl.BlockSpec(bias_block, lambda b, h, t: (layer, jnp.where(t == 0, 0, 1), h, 0, 0)),
            pl.BlockSpec(bias_block, lambda b, h, t: (layer, 1, h, 0, 0)),
            pl.BlockSpec(bias_block, lambda b, h, t: (layer, jnp.where(t == nt - 1, 2, 1), h, 0, 0)),
        ],
        out_specs=pl.BlockSpec((tq, HEAD_DIM), lambda b, h, t: (b * nt + t, h)),
        compiler_params=_params(40),
        name="na_attention",
    )(qkv, qkv, qkv, qkv_c, qkv_c, bias, bias, bias)


def _gq_kernel(q_ref, kl_ref, vl_ref, kc_ref, vc_ref, o_ref, vaug):
    @pl.when(pl.program_id(2) == 0)
    def _():
        _fill_value_ones(vaug, vc_ref, vl_ref)

    q = jnp.concatenate([q_ref[:, 0:HEAD_DIM], q_ref[:, HEAD_DIM:]], axis=0)
    acc = _online_softmax_pv(q, *_flash_chunks(kl_ref, kc_ref, vaug))
    o = acc[:, 0:HEAD_DIM] / acc[:, HEAD_DIM:]
    o_ref[:, 0:HEAD_DIM] = o[0:TQ_FLASH].astype(BF16)
    o_ref[:, HEAD_DIM:] = o[TQ_FLASH:].astype(BF16)


def _gq_attention(qkv, qkv_c):
    nt = SEQ // TQ_FLASH
    kv = lambda col: pl.BlockSpec((SEQ, HEAD_DIM), lambda b, k, t: (b, col + k))
    kvc = lambda col: pl.BlockSpec((CTX_LEN, HEAD_DIM), lambda b, k, t: (b, col + k))
    return pl.pallas_call(
        _gq_kernel,
        out_shape=jax.ShapeDtypeStruct((BATCH * SEQ, 4 * HEAD_DIM), BF16),
        grid=(BATCH, 2, nt),
        in_specs=[
            pl.BlockSpec((TQ_FLASH, 2 * HEAD_DIM), lambda b, k, t: (b * nt + t, GQ_Q // 2 + k)),
            kv(GQ_K), kv(GQ_V), kvc(GQ_K), kvc(GQ_V),
        ],
        out_specs=pl.BlockSpec((TQ_FLASH, 2 * HEAD_DIM), lambda b, k, t: (b * nt + t, k)),
        scratch_shapes=[pltpu.VMEM((CTX_LEN + SEQ, 2 * HEAD_DIM), BF16)],
        compiler_params=_params(48),
        name="gq_attention",
    )(qkv, qkv, qkv, qkv_c, qkv_c)


def _df_kernel(lam_ref, subg_ref, q_ref, kl_ref, vl_ref, kc_ref, vc_ref, o_ref, vaug, *, lam_init):
    @pl.when(pl.program_id(2) == 0)
    def _():
        _fill_value_ones(vaug, vc_ref, vl_ref)

    lam = _diff_lambda(lam_ref, lam_init)
    acc = _online_softmax_pv(_split_diff_query(q_ref[...]), *_flash_chunks(kl_ref, kc_ref, vaug))
    o = acc[:, 0:HEAD_DIM] / acc[:, HEAD_DIM:]
    o = o[0:TQ_FLASH] - lam * o[TQ_FLASH:]
    o_ref[...] = (_rms(o) * (subg_ref[...] * (1.0 - lam_init))).astype(BF16)


def _df_attention(qkv, qkv_c, df_lambda, subg, lam_init):
    nt = SEQ // TQ_FLASH
    kv = lambda col: pl.BlockSpec((SEQ, HEAD_DIM), lambda b, h, t: (b, col + h))
    kvc = lambda col: pl.BlockSpec((CTX_LEN, HEAD_DIM), lambda b, h, t: (b, col + h))
    return pl.pallas_call(
        functools.partial(_df_kernel, lam_init=lam_init),
        out_shape=jax.ShapeDtypeStruct((BATCH * SEQ, 4 * HEAD_DIM), BF16),
        grid=(BATCH, 4, nt),
        in_specs=[
            pl.BlockSpec((4, DF_QK_DIM), lambda b, h, t: (0, 0)),
            pl.BlockSpec((1, HEAD_DIM), lambda b, h, t: (0, 0)),
            pl.BlockSpec((TQ_FLASH, HEAD_DIM), lambda b, h, t: (b * nt + t, DF_Q + h)),
            kv(DF_K), kv(DF_V), kvc(DF_K), kvc(DF_V),
        ],
        out_specs=pl.BlockSpec((TQ_FLASH, HEAD_DIM), lambda b, h, t: (b * nt + t, h)),
        scratch_shapes=[pltpu.VMEM((CTX_LEN + SEQ, 2 * HEAD_DIM), BF16)],
        compiler_params=_params(48),
        name="df_attention",
    )(df_lambda, subg, qkv, qkv, qkv, qkv_c, qkv_c)


def _sw_kernel(sink_ref, q_ref, kl_ref, vl_ref, kc_ref, vc_ref, m0_ref, m1_ref, m2_ref, o_ref):
    step = pl.program_id(2)
    mask_refs = [m0_ref] + [m1_ref] * (SW_SUB - 2) + [m2_ref]
    head = lax.broadcasted_iota(jnp.int32, (2 * TQ_SW, 1), 0) >= TQ_SW
    sink = jnp.where(head, sink_ref[0, 1:2, 0:1], sink_ref[0, 0:1, 0:1]) * LOG2E

    for j in range(SW_SUB):
        t = SW_SUB * step + j
        start = pl.multiple_of(jnp.clip(TQ_SW * t - SW_WINDOW, 0, SEQ - TK_SW), SW_WINDOW)
        keys = _rows(kl_ref[pl.ds(start, TK_SW), :], kc_ref[...])
        values = _rows(vl_ref[pl.ds(start, TK_SW), :], vc_ref[...])
        rows = slice(j * TQ_SW, (j + 1) * TQ_SW)
        q = _rows(q_ref[rows, 0:HEAD_DIM], q_ref[rows, HEAD_DIM:])
        o = _softmax_pv(_dot_nt(q, keys) + mask_refs[j][0], values, extra_logit=sink)
        o_ref[rows, 0:HEAD_DIM] = o[0:TQ_SW].astype(BF16)
        o_ref[rows, HEAD_DIM:] = o[TQ_SW:].astype(BF16)


def _sw_mask_table():
    r = np.arange(TQ_SW)[:, None]
    c = np.arange(TK_SW)[None, :]
    nt = SEQ // TQ_SW
    tables = []
    for t in (0, 1, nt - 1):
        start = min(max(TQ_SW * t - SW_WINDOW, 0), SEQ - TK_SW)
        ok = np.abs((TQ_SW * t + r) - (start + c)) <= SW_WINDOW
        band = np.where(ok, 0.0, NEG_INF).astype(np.float32)
        band = np.concatenate([band, np.zeros((TQ_SW, CTX_LEN), np.float32)], axis=1)
        tables.append(np.concatenate([band, band], axis=0))
    return jnp.asarray(np.stack(tables))


def _sw_attention(qkv, qkv_c, sink):
    tq = SW_SUB * TQ_SW
    nt = SEQ // tq
    mask = _sw_mask_table()
    kv = lambda col: pl.BlockSpec((SEQ, HEAD_DIM), lambda b, k, t: (b, col + k))
    kvc = lambda col: pl.BlockSpec((CTX_LEN, HEAD_DIM), lambda b, k, t: (b, col + k))
    mask_block = (1, 2 * TQ_SW, TK_SW + CTX_LEN)
    return pl.pallas_call(
        _sw_kernel,
        out_shape=jax.ShapeDtypeStruct((BATCH * SEQ, 4 * HEAD_DIM), BF16),
        grid=(BATCH, 2, nt),
        in_specs=[
            pl.BlockSpec((1, 2, HEAD_DIM), lambda b, k, t: (k, 0, 0)),
            pl.BlockSpec((tq, 2 * HEAD_DIM), lambda b, k, t: (b * nt + t, SW_Q // 2 + k)),
            kv(SW_K), kv(SW_V), kvc(SW_K), kvc(SW_V),
            pl.BlockSpec(mask_block, lambda b, k, t: (jnp.where(t == 0, 0, 1), 0, 0)),
            pl.BlockSpec(mask_block, lambda b, k, t: (1, 0, 0)),
            pl.BlockSpec(mask_block, lambda b, k, t: (jnp.where(t == nt - 1, 2, 1), 0, 0)),
        ],
        out_specs=pl.BlockSpec((tq, 2 * HEAD_DIM), lambda b, k, t: (b * nt + t, k)),
        compiler_params=_params(40),
        name="sw_attention",
    )(sink.reshape(2, 2, HEAD_DIM), qkv, qkv, qkv, qkv_c, qkv_c, mask, mask, mask)


def _ctx_attn_kernel(lam_ref, subg_ref, sink_ref, x_ref, o_ref, *, lam_init):
    chunk = lambda j: x_ref[:, j * HEAD_DIM:(j + 1) * HEAD_DIM]

    def put(j, o):
        o_ref[:, j * HEAD_DIM:(j + 1) * HEAD_DIM] = o.astype(BF16)

    lam = _diff_lambda(lam_ref, lam_init)
    for h in range(4):
        put(h, _softmax_pv(_dot_nt(chunk(NA_Q + h), chunk(NA_K + h)), chunk(NA_V + h)))
        put(4 + h, _softmax_pv(_dot_nt(chunk(GQ_Q + h), chunk(GQ_K + h // 2)),
                               chunk(GQ_V + h // 2)))
        q2 = _split_diff_query(chunk(DF_Q + h))
        o = _softmax_pv(_dot_nt(q2, chunk(DF_K + h)), chunk(DF_V + h))
        o = o[0:CTX_LEN] - lam * o[CTX_LEN:]
        put(8 + h, _rms(o) * (subg_ref[...] * (1.0 - lam_init)))
        put(12 + h, _softmax_pv(_dot_nt(chunk(SW_Q + h), chunk(SW_K + h // 2)),
                                chunk(SW_V + h // 2),
                                extra_logit=sink_ref[h:h + 1, 0:1] * LOG2E))


def _ctx_attention(qkv_c, df_lambda, subg, sink, lam_init):
    return pl.pallas_call(
        functools.partial(_ctx_attn_kernel, lam_init=lam_init),
        out_shape=jax.ShapeDtypeStruct((BATCH * CTX_LEN, D_MODEL), BF16),
        grid=(BATCH,),
        in_specs=[
            pl.BlockSpec((4, DF_QK_DIM), lambda b: (0, 0)),
            pl.BlockSpec((1, HEAD_DIM), lambda b: (0, 0)),
            pl.BlockSpec((4, HEAD_DIM), lambda b: (0, 0)),
            pl.BlockSpec((CTX_LEN, IN_WIDTH), lambda b: (b, 0)),
        ],
        out_specs=pl.BlockSpec((CTX_LEN, D_MODEL), lambda b: (b, 0)),
        compiler_params=_params(32),
        name="ctx_attention",
    )(df_lambda, subg, sink, qkv_c)


def _outproj_kernel(y0_ref, y1_ref, y2_ref, y3_ref, h_ref, mod_ref, g_ref, w_ref, o_ref, f_ref):
    gate = mod_ref[0, 2:3, :] * g_ref[1:2, :]
    ffn_gain = _gain(g_ref[2:3, :], mod_ref[0, 4:5, :])
    for c in range(TM_PROJ // ROW_SUB):
        rows = slice(c * ROW_SUB, (c + 1) * ROW_SUB)
        y = jnp.concatenate([y_ref[rows, :] for y_ref in (y0_ref, y1_ref, y2_ref, y3_ref)], axis=1)
        h_new = h_ref[rows, :] + _rms(_dot(y, w_ref[...])) * gate
        o_ref[rows, :] = h_new
        f_ref[rows, :] = (_rms(h_new) * ffn_gain + mod_ref[0, 3:4, :]).astype(BF16)


def _out_projection(ys, y_cols, h, mod, mod_row, g, w_out, layer):
    rows = h.shape[0]
    tm = TM_PROJ
    y_specs = [pl.BlockSpec((tm, 512), lambda i, c=c: (i, c)) for c in y_cols]
    row_block = pl.BlockSpec((tm, D_MODEL), lambda i: (i, 0))
    return pl.pallas_call(
        _outproj_kernel,
        out_shape=(jax.ShapeDtypeStruct((rows, D_MODEL), F32),
                   jax.ShapeDtypeStruct((rows, D_MODEL), BF16)),
        grid=(rows // tm,),
        in_specs=y_specs + [
            row_block,
            pl.BlockSpec((1, 6, D_MODEL), lambda i: (mod_row(i), 0, 0)),
            pl.BlockSpec((4, D_MODEL), lambda i: (0, 0)),
            _resident_layer(layer, (D_MODEL, D_MODEL)),
        ],
        out_specs=(row_block, row_block),
        compiler_params=_params(48),
        name="out_projection",
    )(*ys, h, mod, g, w_out)


def _ffn_up_kernel(f_ref, wg_ref, wu_ref, o_ref, wg_scr, wu_scr):
    @pl.when(pl.program_id(1) == 0)
    def _():
        wg_scr[...] = wg_ref[...].astype(BF16)
        wu_scr[...] = wu_ref[...].astype(BF16)

    f = f_ref[...]
    gate = _dot(f, wg_scr[...])
    up = _dot(f, wu_scr[...])
    o_ref[...] = (gate * jax.nn.sigmoid(gate) * up).astype(BF16)


def _ffn_up(f, w_gate_up, layer):
    rows = f.shape[0]
    tm, tf = min(TM_UP, rows), TF_UP
    nf = D_FF // tf
    return pl.pallas_call(
        _ffn_up_kernel,
        out_shape=jax.ShapeDtypeStruct((rows, D_FF), BF16),
        grid=(nf, rows // tm),
        in_specs=[
            pl.BlockSpec((tm, D_MODEL), lambda k, i: (i, 0)),
            pl.BlockSpec((None, D_MODEL, tf), lambda k, i: (layer, 0, k)),
            pl.BlockSpec((None, D_MODEL, tf), lambda k, i: (layer, 0, nf + k)),
        ],
        out_specs=pl.BlockSpec((tm, tf), lambda k, i: (i, k)),
        scratch_shapes=[pltpu.VMEM((D_MODEL, tf), BF16), pltpu.VMEM((D_MODEL, tf), BF16)],
        compiler_params=_params(48),
        name="ffn_up",
    )(f, w_gate_up, w_gate_up)


def _ffn_down_kernel(a_ref, h_ref, mod_ref, g_ref, w_ref, o_ref):
    gate = mod_ref[0, 5:6, :] * g_ref[3:4, :]
    for c in range(TM_PROJ // ROW_SUB):
        rows = slice(c * ROW_SUB, (c + 1) * ROW_SUB)
        acc = _dot(a_ref[rows, :], w_ref[...])
        o_ref[rows, :] = h_ref[rows, :] + _rms(acc) * gate


def _ffn_down(act, h, mod, mod_row, g, w_down, layer):
    rows = h.shape[0]
    tm = TM_PROJ
    row_block = pl.BlockSpec((tm, D_MODEL), lambda i: (i, 0))
    return pl.pallas_call(
        _ffn_down_kernel,
        out_shape=jax.ShapeDtypeStruct((rows, D_MODEL), F32),
        grid=(rows // tm,),
        in_specs=[
            pl.BlockSpec((tm, D_FF), lambda i: (i, 0)),
            row_block,
            pl.BlockSpec((1, 6, D_MODEL), lambda i: (mod_row(i), 0, 0)),
            pl.BlockSpec((4, D_MODEL), lambda i: (0, 0)),
            _resident_layer(layer, (D_FF, D_MODEL)),
        ],
        out_specs=row_block,
        compiler_params=_params(58),
        name="ffn_down",
    )(act, h, mod, g, w_down)


def _rope_table(dim):
    q = dim // 4
    t = jnp.arange(SEQ)
    pos = jnp.stack([t // GRID_W, t % GRID_W], axis=-1).astype(F32)
    inv = ROPE_THETA ** (-jnp.arange(q, dtype=F32) / q)
    ang = pos[:, :, None] * inv
    cos = jnp.broadcast_to(jnp.cos(ang)[:, :, None, :], (SEQ, 2, 2, q)).reshape(SEQ, dim)
    sin = jnp.sin(ang)
    sin = jnp.stack([-sin, sin], axis=2).reshape(SEQ, dim)
    reps = HEAD_DIM // dim
    return jnp.tile(cos, (1, reps)), jnp.tile(sin, (1, reps))


def kernel(x, c, ctx, c_ctx, w_mod, b_mod, norm_g, w_in, w_out, na_rpb, qk_norm_g,
           df_lambda, df_subln_g, sw_sink, w_gate_up, w_down):
    cvec = jnp.concatenate([c_ctx[None, :], c, jnp.zeros((3, D_MODEL), F32)], axis=0)
    mod_all = _modulation(cvec, w_mod, b_mod).reshape(DEPTH, 8, 6, D_MODEL)
    rope_tab = jnp.stack(_rope_table(HEAD_DIM) + _rope_table(DF_QK_DIM))
    na_bias = _na_bias_table(na_rpb)

    tiles_per_seq = SEQ // TM_PROJ
    lat_row = lambda i: 1 + i // tiles_per_seq
    ctx_row = lambda i: 0

    w_in, w_out, w_down = (w.astype(BF16) for w in (w_in, w_out, w_down))

    h = x.reshape(BATCH * SEQ, D_MODEL)
    hc = ctx.reshape(BATCH * CTX_LEN, D_MODEL)
    for l in range(DEPTH):
        ctx_out = l < DEPTH - 1
        lam_init = 0.8 - 0.6 * math.exp(-0.3 * l)
        mod = mod_all[l]
        g = norm_g[l]
        subg = df_subln_g[l][None, :]
        sink = jnp.broadcast_to(sw_sink[l][:, None], (4, HEAD_DIM))

        qkv = _in_projection(h, mod, lat_row, g, qk_norm_g[l], w_in, l, rope_tab)
        qkv_c = _in_projection(hc, mod, ctx_row, g, qk_norm_g[l], w_in, l, None)

        ys = [
            _na_attention(qkv, qkv_c, na_bias, l),
            _gq_attention(qkv, qkv_c),
            _df_attention(qkv, qkv_c, df_lambda[l], subg, lam_init),
            _sw_attention(qkv, qkv_c, sink),
        ]
        h, f = _out_projection(ys, (0, 0, 0, 0), h, mod, lat_row, g, w_out, l)
        h = _ffn_down(_ffn_up(f, w_gate_up, l), h, mod, lat_row, g, w_down, l)
        if ctx_out:
            yc = _ctx_attention(qkv_c, df_lambda[l], subg, sink, lam_init)
            hc, fc = _out_projection([yc] * 4, (0, 1, 2, 3), hc, mod, ctx_row, g, w_out, l)
            hc = _ffn_down(_ffn_up(fc, w_gate_up, l), hc, mod, ctx_row, g, w_down, l)
    return h.reshape(BATCH, SEQ, D_MODEL)
```

```python
import functools
import math

import jax
import jax.numpy as jnp
import numpy as np
from jax import lax
from jax.experimental import pallas as pl
from jax.experimental.pallas import tpu as pltpu

D_MODEL = 2048
BATCH = 4
SEQ = 4096
DEPTH = 2
GRID_W = 64
GRID_ROWS = SEQ // GRID_W
CTX_LEN = 256
HEAD_DIM = 128
NA_KH = 8
NA_KW = 16
DF_QK_DIM = 64
SW_WINDOW = 128
D_FF = 5632
IN_WIDTH = 5120
ROPE_THETA = 10000.0
EPS = 1e-6
NEG_INF = -1e30
LOG2E = math.log2(math.e)

NA_Q, NA_K, NA_V = 0, 4, 8
GQ_Q, GQ_K, GQ_V = 12, 16, 18
DF_Q, DF_K, DF_V = 20, 24, 28
SW_Q, SW_K, SW_V = 32, 36, 38

F32 = jnp.float32
BF16 = jnp.bfloat16
MIB = 1024 * 1024

TM_PROJ = 512
ROW_SUB = 256
TM_UP = 1024
TF_UP = 512
TN_MOD = 1536
TQ_FLASH = 512
FLASH_CHUNKS = (2048, 2048)
assert sum(FLASH_CHUNKS) == SEQ
NA_QROWS = 4
NA_KROWS = 12
TQ_NA = NA_QROWS * GRID_W
TK_NA = NA_KROWS * GRID_W
TQ_SW = 256
TK_SW = TQ_SW + 2 * SW_WINDOW
NA_SUB = 8
SW_SUB = 4


def _params(vmem_mib):
    return pltpu.CompilerParams(vmem_limit_bytes=vmem_mib * MIB)


def _resident_layer(layer, shape):
    index = (layer,) + (0,) * len(shape)
    return pl.BlockSpec((None,) + tuple(shape), lambda *_: index, pipeline_mode=pl.Buffered(1))


def _gain(g, scale=None):
    return g if scale is None else g * (1.0 + scale)


def _rms(x):
    return x * lax.rsqrt(jnp.mean(x * x, axis=-1, keepdims=True) + EPS)


def _dot(a, b):
    return jnp.dot(a, b, preferred_element_type=F32)


def _dot_nt(a, b):
    return lax.dot_general(a, b, (((1,), (1,)), ((), ())), preferred_element_type=F32)


def _mod_kernel(c_ref, w_ref, b_ref, o_ref):
    c = c_ref[...]
    s = c * jax.nn.sigmoid(c)
    hi = s.astype(BF16)
    lo = (s - hi.astype(F32)).astype(BF16)
    r = _dot(jnp.concatenate([hi, lo], axis=0), w_ref[0].astype(BF16))
    o_ref[0] = r[0:8] + r[8:16] + b_ref[0]


def _modulation(cvec, w_mod, b_mod):
    n = 6 * D_MODEL
    return pl.pallas_call(
        _mod_kernel,
        out_shape=jax.ShapeDtypeStruct((DEPTH, 8, n), F32),
        grid=(DEPTH, n // TN_MOD),
        in_specs=[
            pl.BlockSpec((8, D_MODEL), lambda l, j: (0, 0)),
            pl.BlockSpec((1, D_MODEL, TN_MOD), lambda l, j: (l, 0, j)),
            pl.BlockSpec((1, 1, TN_MOD), lambda l, j: (l, 0, j)),
        ],
        out_specs=pl.BlockSpec((1, 8, TN_MOD), lambda l, j: (l, 0, j)),
        compiler_params=_params(40),
        name="modulation",
    )(cvec, w_mod, b_mod.reshape(DEPTH, 1, n))


def _swap_halves(x, half):
    lane = lax.broadcasted_iota(jnp.int32, x.shape, 1)
    fwd = pltpu.roll(x, HEAD_DIM - half, axis=1)
    bwd = pltpu.roll(x, half, axis=1)
    return jnp.where((lane & (2 * half - 1)) < half, fwd, bwd)


def _inproj_kernel(*refs, rope):
    if rope:
        h_ref, mod_ref, g_ref, qkg_ref, w_ref, rope_ref, o_ref, a_scr = refs
    else:
        h_ref, mod_ref, g_ref, qkg_ref, w_ref, o_ref, a_scr = refs
        rope_ref = None
    a = _rms(h_ref[...]) * _gain(g_ref[0:1, :], mod_ref[0, 1:2, :]) + mod_ref[0, 0:1, :]
    a_scr[...] = a.astype(BF16)

    def rot(x, table, half):
        if rope_ref is None:
            return x
        return x * rope_ref[table] + _swap_halves(x, half) * rope_ref[table + 1]

    scale = HEAD_DIM ** -0.5 * LOG2E
    scale_df = DF_QK_DIM ** -0.5 * LOG2E
    plain = lambda x: x
    groups = [
        (NA_Q, 4, lambda x: x * scale),
        (NA_K, 4, plain),
        (NA_V, 4, plain),
        (GQ_Q, 4, lambda x: rot(_rms(x) * qkg_ref[0:1, :], 0, 32) * scale),
        (GQ_K, 2, lambda x: rot(_rms(x) * qkg_ref[1:2, :], 0, 32)),
        (GQ_V, 2, plain),
        (DF_Q, 4, lambda x: rot(x, 2, 16) * scale_df),
        (DF_K, 4, lambda x: rot(x, 2, 16)),
        (DF_V, 4, plain),
        (SW_Q, 4, lambda x: rot(x, 0, 32) * scale),
        (SW_K, 2, lambda x: rot(x, 0, 32)),
        (SW_V, 2, plain),
    ]
    for first, count, fn in groups:
        c0 = first * HEAD_DIM
        acc = _dot(a_scr[...], w_ref[:, c0:c0 + count * HEAD_DIM])
        for j in range(count):
            x = fn(acc[:, j * HEAD_DIM:(j + 1) * HEAD_DIM])
            o_ref[:, c0 + j * HEAD_DIM:c0 + (j + 1) * HEAD_DIM] = x.astype(BF16)


def _in_projection(h, mod, mod_row, g, qkg, w_in, layer, rope_tab):
    rows = h.shape[0]
    tm = TM_PROJ
    rope = rope_tab is not None
    in_specs = [
        pl.BlockSpec((tm, D_MODEL), lambda i: (i, 0)),
        pl.BlockSpec((1, 6, D_MODEL), lambda i: (mod_row(i), 0, 0)),
        pl.BlockSpec((4, D_MODEL), lambda i: (0, 0)),
        pl.BlockSpec((2, HEAD_DIM), lambda i: (0, 0)),
        _resident_layer(layer, (D_MODEL, IN_WIDTH)),
    ]
    args = [h, mod, g, qkg, w_in]
    if rope:
        tiles_per_seq = SEQ // tm
        in_specs.append(pl.BlockSpec((4, tm, HEAD_DIM), lambda i: (0, i % tiles_per_seq, 0)))
        args.append(rope_tab)
    return pl.pallas_call(
        functools.partial(_inproj_kernel, rope=rope),
        out_shape=jax.ShapeDtypeStruct((rows, IN_WIDTH), BF16),
        grid=(rows // tm,),
        in_specs=in_specs,
        out_specs=pl.BlockSpec((tm, IN_WIDTH), lambda i: (i, 0)),
        scratch_shapes=[pltpu.VMEM((tm, D_MODEL), BF16)],
        compiler_params=_params(56),
        name="in_projection",
    )(*args)


def _softmax_pv(s, v, extra_logit=None):
    m = jnp.max(s, axis=-1, keepdims=True)
    if extra_logit is not None:
        m = jnp.maximum(m, extra_logit)
    p = jnp.exp2(s - m)
    l = jnp.sum(p, axis=-1, keepdims=True)
    if extra_logit is not None:
        l = l + jnp.exp2(extra_logit - m)
    return _dot(p.astype(BF16), v) * (1.0 / l)


def _softmax_pv_blocks(scores, values):
    m = functools.reduce(jnp.maximum, [jnp.max(s, axis=-1, keepdims=True) for s in scores])
    l = None
    o = None
    for s, v in zip(scores, values):
        p = jnp.exp2(s - m)
        li = jnp.sum(p, axis=-1, keepdims=True)
        oi = _dot(p.astype(BF16), v)
        l = li if l is None else l + li
        o = oi if o is None else o + oi
    return o * (1.0 / l)


def _rows(*blocks):
    return jnp.concatenate(blocks, axis=0)


def _online_softmax_pv(q, key_chunks, value_chunks):
    n = len(key_chunks)
    s_next = _dot_nt(q, key_chunks[0]())
    m = None
    acc = None
    for i in range(n):
        s = s_next
        if i + 1 < n:
            s_next = _dot_nt(q, key_chunks[i + 1]())
        mi = jnp.max(s, axis=-1, keepdims=True)
        m_new = mi if m is None else jnp.maximum(m, mi)
        pv = _dot(jnp.exp2(s - m_new).astype(BF16), value_chunks[i]())
        acc = pv if acc is None else acc * jnp.exp2(m - m_new) + pv
        m = m_new
    return acc


def _fill_value_ones(vaug, vc_ref, vl_ref):
    vaug[0:CTX_LEN, 0:HEAD_DIM] = vc_ref[...]
    vaug[CTX_LEN:, 0:HEAD_DIM] = vl_ref[...]
    vaug[:, HEAD_DIM:] = jnp.ones((CTX_LEN + SEQ, HEAD_DIM), BF16)


def _flash_chunks(kl_ref, kc_ref, vaug):
    keys, values = [], []
    start = 0
    for size in FLASH_CHUNKS:
        keys.append(lambda a=start, b=start + size: kl_ref[a:b, :])
        values.append(lambda a=start, b=start + size: vaug[CTX_LEN + a:CTX_LEN + b, :])
        start += size
    keys.append(lambda: kc_ref[...])
    values.append(lambda: vaug[0:CTX_LEN, :])
    return keys, values


def _diff_lambda(lam_ref, lam_init):
    lam = lam_ref[...]
    a = jnp.sum(lam[0:1, :] * lam[1:2, :], axis=-1, keepdims=True)
    b = jnp.sum(lam[2:3, :] * lam[3:4, :], axis=-1, keepdims=True)
    return jnp.exp(a) - jnp.exp(b) + lam_init


def _split_diff_query(q):
    lane = lax.broadcasted_iota(jnp.int32, q.shape, 1)
    zero = jnp.zeros_like(q)
    return jnp.concatenate([jnp.where(lane < DF_QK_DIM, q, zero),
                            jnp.where(lane >= DF_QK_DIM, q, zero)], axis=0)


def _na_kernel(q_ref, kl_ref, vl_ref, kc_ref, vc_ref, b0_ref, b1_ref, b2_ref, o_ref):
    step = pl.program_id(2)
    bias_refs = [b0_ref] + [b1_ref] * (NA_SUB - 2) + [b2_ref]

    def scores(j):
        t = NA_SUB * step + j
        first_row = jnp.clip(NA_QROWS * t - NA_KH // 2, 0, GRID_ROWS - NA_KROWS)
        start = pl.multiple_of(first_row * GRID_W, TQ_NA)
        q = q_ref[j * TQ_NA:(j + 1) * TQ_NA, :]
        s_loc = _dot_nt(q, kl_ref[pl.ds(start, TK_NA), :]) + bias_refs[j][0, 0]
        return start, [s_loc, _dot_nt(q, kc_ref[...])]

    nxt = scores(0)
    for j in range(NA_SUB):
        start, s = nxt
        if j + 1 < NA_SUB:
            nxt = scores(j + 1)
        o = _softmax_pv_blocks(s, [vl_ref[pl.ds(start, TK_NA), :], vc_ref[...]])
        o_ref[j * TQ_NA:(j + 1) * TQ_NA, :] = o.astype(BF16)


def _na_window_geometry(t):
    first_row = min(max(NA_QROWS * t - NA_KH // 2, 0), GRID_ROWS - NA_KROWS)
    r = NA_QROWS * t + np.arange(NA_QROWS)
    kr = first_row + np.arange(NA_KROWS)
    c = np.arange(GRID_W)
    kc = np.arange(GRID_W)
    r0 = np.clip(r - NA_KH // 2, 0, GRID_ROWS - NA_KH)
    c0 = np.clip(c - NA_KW // 2, 0, GRID_W - NA_KW)
    ok_r = (kr[None, :] >= r0[:, None]) & (kr[None, :] < r0[:, None] + NA_KH)
    ok_c = (kc[None, :] >= c0[:, None]) & (kc[None, :] < c0[:, None] + NA_KW)
    dr = np.clip(kr[None, :] - r[:, None] + NA_KH - 1, 0, 2 * NA_KH - 2)
    dc = np.clip(kc[None, :] - c[:, None] + NA_KW - 1, 0, 2 * NA_KW - 2)
    return dr, dc, ok_r, ok_c


def _na_bias_kernel(lo_ref, hi_ref, ok_ref, o_ref, *, row_offset):
    ok = ok_ref[...] > 0.0
    for v in range(3):
        for r in range(NA_QROWS):
            for pair in range(NA_KROWS // 2):
                d0 = row_offset[v][r][2 * pair]
                d1 = row_offset[v][r][2 * pair + 1]
                src = lo_ref[d0:d0 + 1, :] + hi_ref[d1:d1 + 1, :]
                block = pltpu.roll(jnp.broadcast_to(src, (GRID_W, 2 * GRID_W)), 0, 1,
                                   stride=1, stride_axis=0)
                o_ref[v, r * GRID_W:(r + 1) * GRID_W, pair * 2 * GRID_W:(pair + 1) * 2 * GRID_W] = (
                    jnp.where(ok, block, NEG_INF))


def _na_bias_table(na_rpb):
    n_dr, n_dc = 2 * NA_KH - 1, 2 * NA_KW - 1
    half = NA_KW - 1
    rpb = jnp.pad(na_rpb.astype(F32) * LOG2E, ((0, 0), (0, 0), (0, 1), (0, 0)),
                  constant_values=NEG_INF)
    lanes = 2 * GRID_W
    lo = jnp.roll(jnp.pad(rpb, ((0, 0),) * 3 + ((0, lanes - n_dc),)), -half, axis=-1)
    hi = jnp.pad(rpb, ((0, 0),) * 3 + ((GRID_W - half, lanes - n_dc - GRID_W + half),))
    geometry = [_na_window_geometry(t) for t in (0, 1, GRID_ROWS // NA_QROWS - 1)]
    ok_c = geometry[0][3]
    row_offset = [np.where(g[2], g[0], n_dr).tolist() for g in geometry]
    ok = jnp.asarray(np.concatenate([ok_c, ok_c], axis=1).astype(np.float32))
    src_spec = pl.BlockSpec((None, None, n_dr + 1, lanes), lambda l, h: (l, h, 0, 0))
    return pl.pallas_call(
        functools.partial(_na_bias_kernel, row_offset=row_offset),
        out_shape=jax.ShapeDtypeStruct((DEPTH, 3, 4, TQ_NA, TK_NA), F32),
        grid=(DEPTH, 4),
        in_specs=[src_spec, src_spec, pl.BlockSpec((GRID_W, lanes), lambda l, h: (0, 0))],
        out_specs=pl.BlockSpec((None, 3, None, TQ_NA, TK_NA), lambda l, h: (l, 0, h, 0, 0)),
        compiler_params=_params(32),
        name="na_bias",
    )(lo, hi, ok)


def _na_attention(qkv, qkv_c, bias, layer):
    tq = NA_SUB * TQ_NA
    nt = SEQ // tq
    kv = lambda col: pl.BlockSpec((SEQ, HEAD_DIM), lambda b, h, t: (b, col + h))
    kvc = lambda col: pl.BlockSpec((CTX_LEN, HEAD_DIM), lambda b, h, t: (b, col + h))
    bias_block = (None, 1, 1, TQ_NA, TK_NA)
    return pl.pallas_call(
        _na_kernel,
        out_shape=jax.ShapeDtypeStruct((BATCH * SEQ, 4 * HEAD_DIM), BF16),
        grid=(BATCH, 4, nt),
        in_specs=[
            pl.BlockSpec((tq, HEAD_DIM), lambda b, h, t: (b * nt + t, NA_Q + h)),
            kv(NA_K), kv(NA_V), kvc(NA_K), kvc(NA_V),
            pl.BlockSpec(bias_block, lambda b, h, t: (layer, jnp.where(t == 0, 0, 1), h, 0, 0)),
            pl.BlockSpec(bias_block, lambda b, h, t: (layer, 1, h, 0, 0)),
            pl.BlockSpec(bias_block, lambda b, h, t: (layer, jnp.where(t == nt - 1, 2, 1), h, 0, 0)),
        ],
        out_specs=pl.BlockSpec((tq, HEAD_DIM), lambda b, h, t: (b * nt + t, h)),
        compiler_params=_params(40),
        name="na_attention",
    )(qkv, qkv, qkv, qkv_c, qkv_c, bias, bias, bias)


def _gq_kernel(q_ref, kl_ref, vl_ref, kc_ref, vc_ref, o_ref, vaug):
    @pl.when(pl.program_id(2) == 0)
    def _():
        _fill_value_ones(vaug, vc_ref, vl_ref)

    q = jnp.concatenate([q_ref[:, 0:HEAD_DIM], q_ref[:, HEAD_DIM:]], axis=0)
    acc = _online_softmax_pv(q, *_flash_chunks(kl_ref, kc_ref, vaug))
    o = acc[:, 0:HEAD_DIM] / acc[:, HEAD_DIM:]
    o_ref[:, 0:HEAD_DIM] = o[0:TQ_FLASH].astype(BF16)
    o_ref[:, HEAD_DIM:] = o[TQ_FLASH:].astype(BF16)


def _gq_attention(qkv, qkv_c):
    nt = SEQ // TQ_FLASH
    kv = lambda col: pl.BlockSpec((SEQ, HEAD_DIM), lambda b, k, t: (b, col + k))
    kvc = lambda col: pl.BlockSpec((CTX_LEN, HEAD_DIM), lambda b, k, t: (b, col + k))
    return pl.pallas_call(
        _gq_kernel,
        out_shape=jax.ShapeDtypeStruct((BATCH * SEQ, 4 * HEAD_DIM), BF16),
        grid=(BATCH, 2, nt),
        in_specs=[
            pl.BlockSpec((TQ_FLASH, 2 * HEAD_DIM), lambda b, k, t: (b * nt + t, GQ_Q // 2 + k)),
            kv(GQ_K), kv(GQ_V), kvc(GQ_K), kvc(GQ_V),
        ],
        out_specs=pl.BlockSpec((TQ_FLASH, 2 * HEAD_DIM), lambda b, k, t: (b * nt + t, k)),
        scratch_shapes=[pltpu.VMEM((CTX_LEN + SEQ, 2 * HEAD_DIM), BF16)],
        compiler_params=_params(48),
        name="gq_attention",
    )(qkv, qkv, qkv, qkv_c, qkv_c)


def _df_kernel(lam_ref, subg_ref, q_ref, kl_ref, vl_ref, kc_ref, vc_ref, o_ref, vaug, *, lam_init):
    @pl.when(pl.program_id(2) == 0)
    def _():
        _fill_value_ones(vaug, vc_ref, vl_ref)

    lam = _diff_lambda(lam_ref, lam_init)
    acc = _online_softmax_pv(_split_diff_query(q_ref[...]), *_flash_chunks(kl_ref, kc_ref, vaug))
    o = acc[:, 0:HEAD_DIM] / acc[:, HEAD_DIM:]
    o = o[0:TQ_FLASH] - lam * o[TQ_FLASH:]
    o_ref[...] = (_rms(o) * (subg_ref[...] * (1.0 - lam_init))).astype(BF16)


def _df_attention(qkv, qkv_c, df_lambda, subg, lam_init):
    nt = SEQ // TQ_FLASH
    kv = lambda col: pl.BlockSpec((SEQ, HEAD_DIM), lambda b, h, t: (b, col + h))
    kvc = lambda col: pl.BlockSpec((CTX_LEN, HEAD_DIM), lambda b, h, t: (b, col + h))
    return pl.pallas_call(
        functools.partial(_df_kernel, lam_init=lam_init),
        out_shape=jax.ShapeDtypeStruct((BATCH * SEQ, 4 * HEAD_DIM), BF16),
        grid=(BATCH, 4, nt),
        in_specs=[
            pl.BlockSpec((4, DF_QK_DIM), lambda b, h, t: (0, 0)),
            pl.BlockSpec((1, HEAD_DIM), lambda b, h, t: (0, 0)),
            pl.BlockSpec((TQ_FLASH, HEAD_DIM), lambda b, h, t: (b * nt + t, DF_Q + h)),
            kv(DF_K), kv(DF_V), kvc(DF_K), kvc(DF_V),
        ],
        out_specs=pl.BlockSpec((TQ_FLASH, HEAD_DIM), lambda b, h, t: (b * nt + t, h)),
        scratch_shapes=[pltpu.VMEM((CTX_LEN + SEQ, 2 * HEAD_DIM), BF16)],
        compiler_params=_params(48),
        name="df_attention",
    )(df_lambda, subg, qkv, qkv, qkv, qkv_c, qkv_c)


def _sw_kernel(sink_ref, q_ref, kl_ref, vl_ref, kc_ref, vc_ref, m0_ref, m1_ref, m2_ref, o_ref):
    step = pl.program_id(2)
    mask_refs = [m0_ref] + [m1_ref] * (SW_SUB - 2) + [m2_ref]
    head = lax.broadcasted_iota(jnp.int32, (2 * TQ_SW, 1), 0) >= TQ_SW
    sink = jnp.where(head, sink_ref[0, 1:2, 0:1], sink_ref[0, 0:1, 0:1]) * LOG2E

    for j in range(SW_SUB):
        t = SW_SUB * step + j
        start = pl.multiple_of(jnp.clip(TQ_SW * t - SW_WINDOW, 0, SEQ - TK_SW), SW_WINDOW)
        keys = _rows(kl_ref[pl.ds(start, TK_SW), :], kc_ref[...])
        values = _rows(vl_ref[pl.ds(start, TK_SW), :], vc_ref[...])
        rows = slice(j * TQ_SW, (j + 1) * TQ_SW)
        q = _rows(q_ref[rows, 0:HEAD_DIM], q_ref[rows, HEAD_DIM:])
        o = _softmax_pv(_dot_nt(q, keys) + mask_refs[j][0], values, extra_logit=sink)
        o_ref[rows, 0:HEAD_DIM] = o[0:TQ_SW].astype(BF16)
        o_ref[rows, HEAD_DIM:] = o[TQ_SW:].astype(BF16)


def _sw_mask_table():
    r = np.arange(TQ_SW)[:, None]
    c = np.arange(TK_SW)[None, :]
    nt = SEQ // TQ_SW
    tables = []
    for t in (0, 1, nt - 1):
        start = min(max(TQ_SW * t - SW_WINDOW, 0), SEQ - TK_SW)
        ok = np.abs((TQ_SW * t + r) - (start + c)) <= SW_WINDOW
        band = np.where(ok, 0.0, NEG_INF).astype(np.float32)
        band = np.concatenate([band, np.zeros((TQ_SW, CTX_LEN), np.float32)], axis=1)
        tables.append(np.concatenate([band, band], axis=0))
    return jnp.asarray(np.stack(tables))


def _sw_attention(qkv, qkv_c, sink):
    tq = SW_SUB * TQ_SW
    nt = SEQ // tq
    mask = _sw_mask_table()
    kv = lambda col: pl.BlockSpec((SEQ, HEAD_DIM), lambda b, k, t: (b, col + k))
    kvc = lambda col: pl.BlockSpec((CTX_LEN, HEAD_DIM), lambda b, k, t: (b, col + k))
    mask_block = (1, 2 * TQ_SW, TK_SW + CTX_LEN)
    return pl.pallas_call(
        _sw_kernel,
        out_shape=jax.ShapeDtypeStruct((BATCH * SEQ, 4 * HEAD_DIM), BF16),
        grid=(BATCH, 2, nt),
        in_specs=[
            pl.BlockSpec((1, 2, HEAD_DIM), lambda b, k, t: (k, 0, 0)),
            pl.BlockSpec((tq, 2 * HEAD_DIM), lambda b, k, t: (b * nt + t, SW_Q // 2 + k)),
            kv(SW_K), kv(SW_V), kvc(SW_K), kvc(SW_V),
            pl.BlockSpec(mask_block, lambda b, k, t: (jnp.where(t == 0, 0, 1), 0, 0)),
            pl.BlockSpec(mask_block, lambda b, k, t: (1, 0, 0)),
            pl.BlockSpec(mask_block, lambda b, k, t: (jnp.where(t == nt - 1, 2, 1), 0, 0)),
        ],
        out_specs=pl.BlockSpec((tq, 2 * HEAD_DIM), lambda b, k, t: (b * nt + t, k)),
        compiler_params=_params(40),
        name="sw_attention",
    )(sink.reshape(2, 2, HEAD_DIM), qkv, qkv, qkv, qkv_c, qkv_c, mask, mask, mask)


def _ctx_attn_kernel(lam_ref, subg_ref, sink_ref, x_ref, o_ref, *, lam_init):
    chunk = lambda j: x_ref[:, j * HEAD_DIM:(j + 1) * HEAD_DIM]

    def put(j, o):
        o_ref[:, j * HEAD_DIM:(j + 1) * HEAD_DIM] = o.astype(BF16)

    lam = _diff_lambda(lam_ref, lam_init)
    for h in range(4):
        put(h, _softmax_pv(_dot_nt(chunk(NA_Q + h), chunk(NA_K + h)), chunk(NA_V + h)))
        put(4 + h, _softmax_pv(_dot_nt(chunk(GQ_Q + h), chunk(GQ_K + h // 2)),
                               chunk(GQ_V + h // 2)))
        q2 = _split_diff_query(chunk(DF_Q + h))
        o = _softmax_pv(_dot_nt(q2, chunk(DF_K + h)), chunk(DF_V + h))
        o = o[0:CTX_LEN] - lam * o[CTX_LEN:]
        put(8 + h, _rms(o) * (subg_ref[...] * (1.0 - lam_init)))
        put(12 + h, _softmax_pv(_dot_nt(chunk(SW_Q + h), chunk(SW_K + h // 2)),
                                chunk(SW_V + h // 2),
                                extra_logit=sink_ref[h:h + 1, 0:1] * LOG2E))


def _ctx_attention(qkv_c, df_lambda, subg, sink, lam_init):
    return pl.pallas_call(
        functools.partial(_ctx_attn_kernel, lam_init=lam_init),
        out_shape=jax.ShapeDtypeStruct((BATCH * CTX_LEN, D_MODEL), BF16),
        grid=(BATCH,),
        in_specs=[
            pl.BlockSpec((4, DF_QK_DIM), lambda b: (0, 0)),
            pl.BlockSpec((1, HEAD_DIM), lambda b: (0, 0)),
            pl.BlockSpec((4, HEAD_DIM), lambda b: (0, 0)),
            pl.BlockSpec((CTX_LEN, IN_WIDTH), lambda b: (b, 0)),
        ],
        out_specs=pl.BlockSpec((CTX_LEN, D_MODEL), lambda b: (b, 0)),
        compiler_params=_params(32),
        name="ctx_attention",
    )(df_lambda, subg, sink, qkv_c)


def _outproj_kernel(y0_ref, y1_ref, y2_ref, y3_ref, h_ref, mod_ref, g_ref, w32_ref, o_ref, f_ref,
                    w_ref):
    @pl.when(pl.program_id(0) == 0)
    def _():
        w_ref[...] = w32_ref[...].astype(BF16)

    gate = mod_ref[0, 2:3, :] * g_ref[1:2, :]
    ffn_gain = _gain(g_ref[2:3, :], mod_ref[0, 4:5, :])
    for c in range(TM_PROJ // ROW_SUB):
        rows = slice(c * ROW_SUB, (c + 1) * ROW_SUB)
        y = jnp.concatenate([y_ref[rows, :] for y_ref in (y0_ref, y1_ref, y2_ref, y3_ref)], axis=1)
        h_new = h_ref[rows, :] + _rms(_dot(y, w_ref[...])) * gate
        o_ref[rows, :] = h_new
        f_ref[rows, :] = (_rms(h_new) * ffn_gain + mod_ref[0, 3:4, :]).astype(BF16)


def _out_projection(ys, y_cols, h, mod, mod_row, g, w_out, layer):
    rows = h.shape[0]
    tm = TM_PROJ
    y_specs = [pl.BlockSpec((tm, 512), lambda i, c=c: (i, c)) for c in y_cols]
    row_block = pl.BlockSpec((tm, D_MODEL), lambda i: (i, 0))
    return pl.pallas_call(
        _outproj_kernel,
        out_shape=(jax.ShapeDtypeStruct((rows, D_MODEL), F32),
                   jax.ShapeDtypeStruct((rows, D_MODEL), BF16)),
        grid=(rows // tm,),
        in_specs=y_specs + [
            row_block,
            pl.BlockSpec((1, 6, D_MODEL), lambda i: (mod_row(i), 0, 0)),
            pl.BlockSpec((4, D_MODEL), lambda i: (0, 0)),
            _resident_layer(layer, (D_MODEL, D_MODEL)),
        ],
        out_specs=(row_block, row_block),
        scratch_shapes=[pltpu.VMEM((D_MODEL, D_MODEL), BF16)],
        compiler_params=_params(56),
        name="out_projection",
    )(*ys, h, mod, g, w_out)


def _ffn_up_kernel(f_ref, wg_ref, wu_ref, o_ref, wg_scr, wu_scr):
    @pl.when(pl.program_id(1) == 0)
    def _():
        wg_scr[...] = wg_ref[...].astype(BF16)
        wu_scr[...] = wu_ref[...].astype(BF16)

    f = f_ref[...]
    gate = _dot(f, wg_scr[...])
    up = _dot(f, wu_scr[...])
    o_ref[...] = (gate * jax.nn.sigmoid(gate) * up).astype(BF16)


def _ffn_up(f, w_gate_up, layer):
    rows = f.shape[0]
    tm, tf = min(TM_UP, rows), TF_UP
    nf = D_FF // tf
    return pl.pallas_call(
        _ffn_up_kernel,
        out_shape=jax.ShapeDtypeStruct((rows, D_FF), BF16),
        grid=(nf, rows // tm),
        in_specs=[
            pl.BlockSpec((tm, D_MODEL), lambda k, i: (i, 0)),
            pl.BlockSpec((None, D_MODEL, tf), lambda k, i: (layer, 0, k)),
            pl.BlockSpec((None, D_MODEL, tf), lambda k, i: (layer, 0, nf + k)),
        ],
        out_specs=pl.BlockSpec((tm, tf), lambda k, i: (i, k)),
        scratch_shapes=[pltpu.VMEM((D_MODEL, tf), BF16), pltpu.VMEM((D_MODEL, tf), BF16)],
        compiler_params=_params(48),
        name="ffn_up",
    )(f, w_gate_up, w_gate_up)


def _ffn_down_kernel(a_ref, h_ref, mod_ref, g_ref, w_ref, o_ref):
    gate = mod_ref[0, 5:6, :] * g_ref[3:4, :]
    for c in range(TM_PROJ // ROW_SUB):
        rows = slice(c * ROW_SUB, (c + 1) * ROW_SUB)
        acc = _dot(a_ref[rows, :], w_ref[...])
        o_ref[rows, :] = h_ref[rows, :] + _rms(acc) * gate


def _ffn_down(act, h, mod, mod_row, g, w_down, layer):
    rows = h.shape[0]
    tm = TM_PROJ
    row_block = pl.BlockSpec((tm, D_MODEL), lambda i: (i, 0))
    return pl.pallas_call(
        _ffn_down_kernel,
        out_shape=jax.ShapeDtypeStruct((rows, D_MODEL), F32),
        grid=(rows // tm,),
        in_specs=[
            pl.BlockSpec((tm, D_FF), lambda i: (i, 0)),
            row_block,
            pl.BlockSpec((1, 6, D_MODEL), lambda i: (mod_row(i), 0, 0)),
            pl.BlockSpec((4, D_MODEL), lambda i: (0, 0)),
            _resident_layer(layer, (D_FF, D_MODEL)),
        ],
        out_specs=row_block,
        compiler_params=_params(58),
        name="ffn_down",
    )(act, h, mod, g, w_down)


def _rope_table(dim):
    q = dim // 4
    t = jnp.arange(SEQ)
    pos = jnp.stack([t // GRID_W, t % GRID_W], axis=-1).astype(F32)
    inv = ROPE_THETA ** (-jnp.arange(q, dtype=F32) / q)
    ang = pos[:, :, None] * inv
    cos = jnp.broadcast_to(jnp.cos(ang)[:, :, None, :], (SEQ, 2, 2, q)).reshape(SEQ, dim)
    sin = jnp.sin(ang)
    sin = jnp.stack([-sin, sin], axis=2).reshape(SEQ, dim)
    reps = HEAD_DIM // dim
    return jnp.tile(cos, (1, reps)), jnp.tile(sin, (1, reps))


def kernel(x, c, ctx, c_ctx, w_mod, b_mod, norm_g, w_in, w_out, na_rpb, qk_norm_g,
           df_lambda, df_subln_g, sw_sink, w_gate_up, w_down):
    cvec = jnp.concatenate([c_ctx[None, :], c, jnp.zeros((3, D_MODEL), F32)], axis=0)
    mod_all = _modulation(cvec, w_mod, b_mod).reshape(DEPTH, 8, 6, D_MODEL)
    rope_tab = jnp.stack(_rope_table(HEAD_DIM) + _rope_table(DF_QK_DIM))
    na_bias = _na_bias_table(na_rpb)

    tiles_per_seq = SEQ // TM_PROJ
    lat_row = lambda i: 1 + i // tiles_per_seq
    ctx_row = lambda i: 0

    w_in, w_down = w_in.astype(BF16), w_down.astype(BF16)

    h = x.reshape(BATCH * SEQ, D_MODEL)
    hc = ctx.reshape(BATCH * CTX_LEN, D_MODEL)
    for l in range(DEPTH):
        ctx_out = l < DEPTH - 1
        lam_init = 0.8 - 0.6 * math.exp(-0.3 * l)
        mod = mod_all[l]
        g = norm_g[l]
        subg = df_subln_g[l][None, :]
        sink = jnp.broadcast_to(sw_sink[l][:, None], (4, HEAD_DIM))

        qkv = _in_projection(h, mod, lat_row, g, qk_norm_g[l], w_in, l, rope_tab)
        qkv_c = _in_projection(hc, mod, ctx_row, g, qk_norm_g[l], w_in, l, None)

        ys = [
            _na_attention(qkv, qkv_c, na_bias, l),
            _gq_attention(qkv, qkv_c),
            _df_attention(qkv, qkv_c, df_lambda[l], subg, lam_init),
            _sw_attention(qkv, qkv_c, sink),
        ]
        h, f = _out_projection(ys, (0, 0, 0, 0), h, mod, lat_row, g, w_out, l)
        h = _ffn_down(_ffn_up(f, w_gate_up, l), h, mod, lat_row, g, w_down, l)
        if ctx_out:
            yc = _ctx_attention(qkv_c, df_lambda[l], subg, sink, lam_init)
            hc, fc = _out_projection([yc] * 4, (0, 1, 2, 3), hc, mod, ctx_row, g, w_out, l)
            hc = _ffn_down(_ffn_up(fc, w_gate_up, l), hc, mod, ctx_row, g, w_down, l)
    return h.reshape(BATCH, SEQ, D_MODEL)
```

```python
import functools
import math

import jax
import jax.numpy as jnp
import numpy as np
from jax import lax
from jax.experimental import pallas as pl
from jax.experimental.pallas import tpu as pltpu

D_MODEL = 2048
BATCH = 4
SEQ = 4096
DEPTH = 2
GRID_W = 64
GRID_ROWS = SEQ // GRID_W
CTX_LEN = 256
HEAD_DIM = 128
NA_KH = 8
NA_KW = 16
DF_QK_DIM = 64
SW_WINDOW = 128
D_FF = 5632
IN_WIDTH = 5120
ROPE_THETA = 10000.0
EPS = 1e-6
NEG_INF = -1e30
LOG2E = math.log2(math.e)

NA_Q, NA_K, NA_V = 0, 4, 8
GQ_Q, GQ_K, GQ_V = 12, 16, 18
DF_Q, DF_K, DF_V = 20, 24, 28
SW_Q, SW_K, SW_V = 32, 36, 38

F32 = jnp.float32
BF16 = jnp.bfloat16
MIB = 1024 * 1024

TM_PROJ = 512
ROW_SUB = 256
TM_UP = 1024
TF_UP = 512
TN_MOD = 1536
TQ_FLASH = 512
FLASH_CHUNKS = (2048, 2048)
assert sum(FLASH_CHUNKS) == SEQ
NA_QROWS = 4
NA_KROWS = 12
TQ_NA = NA_QROWS * GRID_W
TK_NA = NA_KROWS * GRID_W
TQ_SW = 256
TK_SW = TQ_SW + 2 * SW_WINDOW
NA_SUB = 16
SW_SUB = 4
MOD_ROWS = 8
MIX_COLS = 4 * HEAD_DIM
ROT_HALF = HEAD_DIM // 4
ROT_HALF_DF = DF_QK_DIM // 4

VMEM_LIMIT_MIB = {
    "modulation": 40, "in_projection": 56, "na_bias": 32, "na_attention": 40,
    "gq_attention": 48, "df_attention": 48, "sw_attention": 40, "ctx_attention": 32,
    "out_projection": 56, "ffn_up": 48, "ffn_down": 58,
}


def _call(kernel, name, **kwargs):
    params = pltpu.CompilerParams(vmem_limit_bytes=VMEM_LIMIT_MIB[name] * MIB)
    return pl.pallas_call(kernel, name=name, compiler_params=params, **kwargs)


def _resident_layer(layer, shape):
    index = (layer,) + (0,) * len(shape)
    return pl.BlockSpec((None,) + tuple(shape), lambda *_: index, pipeline_mode=pl.Buffered(1))


def _gain(g, scale=None):
    return g if scale is None else g * (1.0 + scale)


def _rms(x):
    return x * lax.rsqrt(jnp.mean(x * x, axis=-1, keepdims=True) + EPS)


def _dot(a, b):
    return jnp.dot(a, b, preferred_element_type=F32)


def _dot_nt(a, b):
    return lax.dot_general(a, b, (((1,), (1,)), ((), ())), preferred_element_type=F32)


def _mod_kernel(c_ref, w_ref, b_ref, o_ref):
    c = c_ref[...]
    s = c * jax.nn.sigmoid(c)
    hi = s.astype(BF16)
    lo = (s - hi.astype(F32)).astype(BF16)
    r = _dot(jnp.concatenate([hi, lo], axis=0), w_ref[0].astype(BF16))
    o_ref[0] = r[0:MOD_ROWS] + r[MOD_ROWS:] + b_ref[0]


def _modulation(cvec, w_mod, b_mod):
    n = 6 * D_MODEL
    return _call(
        _mod_kernel, "modulation",
        out_shape=jax.ShapeDtypeStruct((DEPTH, MOD_ROWS, n), F32),
        grid=(DEPTH, n // TN_MOD),
        in_specs=[
            pl.BlockSpec((MOD_ROWS, D_MODEL), lambda l, j: (0, 0)),
            pl.BlockSpec((1, D_MODEL, TN_MOD), lambda l, j: (l, 0, j)),
            pl.BlockSpec((1, 1, TN_MOD), lambda l, j: (l, 0, j)),
        ],
        out_specs=pl.BlockSpec((1, MOD_ROWS, TN_MOD), lambda l, j: (l, 0, j)),
    )(cvec, w_mod, b_mod.reshape(DEPTH, 1, n))


def _swap_halves(x, half):
    lane = lax.broadcasted_iota(jnp.int32, x.shape, 1)
    fwd = pltpu.roll(x, HEAD_DIM - half, axis=1)
    bwd = pltpu.roll(x, half, axis=1)
    return jnp.where((lane & (2 * half - 1)) < half, fwd, bwd)


def _inproj_kernel(*refs, rope):
    if rope:
        h_ref, mod_ref, g_ref, qkg_ref, w_ref, rope_ref, o_ref, a_scr = refs
    else:
        h_ref, mod_ref, g_ref, qkg_ref, w_ref, o_ref, a_scr = refs
        rope_ref = None
    a = _rms(h_ref[...]) * _gain(g_ref[0:1, :], mod_ref[0, 1:2, :]) + mod_ref[0, 0:1, :]
    a_scr[...] = a.astype(BF16)

    def rot(x, table, half):
        if rope_ref is None:
            return x
        return x * rope_ref[table] + _swap_halves(x, half) * rope_ref[table + 1]

    scale = HEAD_DIM ** -0.5 * LOG2E
    scale_df = DF_QK_DIM ** -0.5 * LOG2E
    plain = lambda x: x
    groups = [
        (NA_Q, 4, lambda x: x * scale),
        (NA_K, 4, plain),
        (NA_V, 4, plain),
        (GQ_Q, 4, lambda x: rot(_rms(x) * qkg_ref[0:1, :], 0, ROT_HALF) * scale),
        (GQ_K, 2, lambda x: rot(_rms(x) * qkg_ref[1:2, :], 0, ROT_HALF)),
        (GQ_V, 2, plain),
        (DF_Q, 4, lambda x: rot(x, 2, ROT_HALF_DF) * scale_df),
        (DF_K, 4, lambda x: rot(x, 2, ROT_HALF_DF)),
        (DF_V, 4, plain),
        (SW_Q, 4, lambda x: rot(x, 0, ROT_HALF) * scale),
        (SW_K, 2, lambda x: rot(x, 0, ROT_HALF)),
        (SW_V, 2, plain),
    ]
    for first, count, fn in groups:
        c0 = first * HEAD_DIM
        acc = _dot(a_scr[...], w_ref[:, c0:c0 + count * HEAD_DIM])
        for j in range(count):
            x = fn(acc[:, j * HEAD_DIM:(j + 1) * HEAD_DIM])
            o_ref[:, c0 + j * HEAD_DIM:c0 + (j + 1) * HEAD_DIM] = x.astype(BF16)


def _in_projection(h, mod, mod_row, g, qkg, w_in, layer, rope_tab):
    rows = h.shape[0]
    tm = TM_PROJ
    rope = rope_tab is not None
    in_specs = [
        pl.BlockSpec((tm, D_MODEL), lambda i: (i, 0)),
        pl.BlockSpec((1, 6, D_MODEL), lambda i: (mod_row(i), 0, 0)),
        pl.BlockSpec((4, D_MODEL), lambda i: (0, 0)),
        pl.BlockSpec((2, HEAD_DIM), lambda i: (0, 0)),
        _resident_layer(layer, (D_MODEL, IN_WIDTH)),
    ]
    args = [h, mod, g, qkg, w_in]
    if rope:
        tiles_per_seq = SEQ // tm
        in_specs.append(pl.BlockSpec((4, tm, HEAD_DIM), lambda i: (0, i % tiles_per_seq, 0)))
        args.append(rope_tab)
    return _call(
        functools.partial(_inproj_kernel, rope=rope), "in_projection",
        out_shape=jax.ShapeDtypeStruct((rows, IN_WIDTH), BF16),
        grid=(rows // tm,),
        in_specs=in_specs,
        out_specs=pl.BlockSpec((tm, IN_WIDTH), lambda i: (i, 0)),
        scratch_shapes=[pltpu.VMEM((tm, D_MODEL), BF16)],
    )(*args)


def _softmax_pv(s, v, extra_logit=None):
    m = jnp.max(s, axis=-1, keepdims=True)
    if extra_logit is not None:
        m = jnp.maximum(m, extra_logit)
    p = jnp.exp2(s - m)
    l = jnp.sum(p, axis=-1, keepdims=True)
    if extra_logit is not None:
        l = l + jnp.exp2(extra_logit - m)
    return _dot(p.astype(BF16), v) * (1.0 / l)


def _softmax_pv_blocks(scores, values):
    m = functools.reduce(jnp.maximum, [jnp.max(s, axis=-1, keepdims=True) for s in scores])
    l = None
    o = None
    for s, v in zip(scores, values):
        p = jnp.exp2(s - m)
        li = jnp.sum(p, axis=-1, keepdims=True)
        oi = _dot(p.astype(BF16), v)
        l = li if l is None else l + li
        o = oi if o is None else o + oi
    return o * (1.0 / l)


def _rows(*blocks):
    return jnp.concatenate(blocks, axis=0)


def _online_softmax_pv(q, key_chunks, value_chunks):
    n = len(key_chunks)
    s_next = _dot_nt(q, key_chunks[0]())
    m = None
    acc = None
    for i in range(n):
        s = s_next
        if i + 1 < n:
            s_next = _dot_nt(q, key_chunks[i + 1]())
        mi = jnp.max(s, axis=-1, keepdims=True)
        m_new = mi if m is None else jnp.maximum(m, mi)
        pv = _dot(jnp.exp2(s - m_new).astype(BF16), value_chunks[i]())
        acc = pv if acc is None else acc * jnp.exp2(m - m_new) + pv
        m = m_new
    return acc


def _fill_value_ones(vaug, vc_ref, vl_ref):
    vaug[0:CTX_LEN, 0:HEAD_DIM] = vc_ref[...]
    vaug[CTX_LEN:, 0:HEAD_DIM] = vl_ref[...]
    vaug[:, HEAD_DIM:] = jnp.ones((CTX_LEN + SEQ, HEAD_DIM), BF16)


def _flash_chunks(kl_ref, kc_ref, vaug):
    keys, values = [], []
    start = 0
    for size in FLASH_CHUNKS:
        keys.append(lambda a=start, b=start + size: kl_ref[a:b, :])
        values.append(lambda a=start, b=start + size: vaug[CTX_LEN + a:CTX_LEN + b, :])
        start += size
    keys.append(lambda: kc_ref[...])
    values.append(lambda: vaug[0:CTX_LEN, :])
    return keys, values


def _diff_lambda(lam_ref, lam_init):
    lam = lam_ref[...]
    a = jnp.sum(lam[0:1, :] * lam[1:2, :], axis=-1, keepdims=True)
    b = jnp.sum(lam[2:3, :] * lam[3:4, :], axis=-1, keepdims=True)
    return jnp.exp(a) - jnp.exp(b) + lam_init


def _split_diff_query(q):
    lane = lax.broadcasted_iota(jnp.int32, q.shape, 1)
    zero = jnp.zeros_like(q)
    return jnp.concatenate([jnp.where(lane < DF_QK_DIM, q, zero),
                            jnp.where(lane >= DF_QK_DIM, q, zero)], axis=0)


def _na_kernel(q_ref, kl_ref, vl_ref, kc_ref, vc_ref, b0_ref, b1_ref, b2_ref, o_ref):
    step = pl.program_id(2)
    bias_refs = [b0_ref] + [b1_ref] * (NA_SUB - 2) + [b2_ref]

    def scores(j):
        t = NA_SUB * step + j
        first_row = jnp.clip(NA_QROWS * t - NA_KH // 2, 0, GRID_ROWS - NA_KROWS)
        start = pl.multiple_of(first_row * GRID_W, TQ_NA)
        q = q_ref[j * TQ_NA:(j + 1) * TQ_NA, :]
        s_loc = _dot_nt(q, kl_ref[pl.ds(start, TK_NA), :]) + bias_refs[j][0, 0]
        return start, [s_loc, _dot_nt(q, kc_ref[...])]

    nxt = scores(0)
    for j in range(NA_SUB):
        start, s = nxt
        if j + 1 < NA_SUB:
            nxt = scores(j + 1)
        o = _softmax_pv_blocks(s, [vl_ref[pl.ds(start, TK_NA), :], vc_ref[...]])
        o_ref[j * TQ_NA:(j + 1) * TQ_NA, :] = o.astype(BF16)


def _na_window_geometry(t):
    first_row = min(max(NA_QROWS * t - NA_KH // 2, 0), GRID_ROWS - NA_KROWS)
    r = NA_QROWS * t + np.arange(NA_QROWS)
    kr = first_row + np.arange(NA_KROWS)
    c = np.arange(GRID_W)
    kc = np.arange(GRID_W)
    r0 = np.clip(r - NA_KH // 2, 0, GRID_ROWS - NA_KH)
    c0 = np.clip(c - NA_KW // 2, 0, GRID_W - NA_KW)
    ok_r = (kr[None, :] >= r0[:, None]) & (kr[None, :] < r0[:, None] + NA_KH)
    ok_c = (kc[None, :] >= c0[:, None]) & (kc[None, :] < c0[:, None] + NA_KW)
    dr = np.clip(kr[None, :] - r[:, None] + NA_KH - 1, 0, 2 * NA_KH - 2)
    dc = np.clip(kc[None, :] - c[:, None] + NA_KW - 1, 0, 2 * NA_KW - 2)
    return dr, dc, ok_r, ok_c


def _na_bias_kernel(lo_ref, hi_ref, ok_ref, o_ref, *, row_offset):
    ok = ok_ref[...] > 0.0
    for v in range(3):
        for r in range(NA_QROWS):
            for pair in range(NA_KROWS // 2):
                d0 = row_offset[v][r][2 * pair]
                d1 = row_offset[v][r][2 * pair + 1]
                src = lo_ref[d0:d0 + 1, :] + hi_ref[d1:d1 + 1, :]
                block = pltpu.roll(jnp.broadcast_to(src, (GRID_W, 2 * GRID_W)), 0, 1,
                                   stride=1, stride_axis=0)
                o_ref[v, r * GRID_W:(r + 1) * GRID_W, pair * 2 * GRID_W:(pair + 1) * 2 * GRID_W] = (
                    jnp.where(ok, block, NEG_INF))


def _na_bias_table(na_rpb):
    n_dr, n_dc = 2 * NA_KH - 1, 2 * NA_KW - 1
    half = NA_KW - 1
    rpb = jnp.pad(na_rpb.astype(F32) * LOG2E, ((0, 0), (0, 0), (0, 1), (0, 0)),
                  constant_values=NEG_INF)
    lanes = 2 * GRID_W
    lo = jnp.roll(jnp.pad(rpb, ((0, 0),) * 3 + ((0, lanes - n_dc),)), -half, axis=-1)
    hi = jnp.pad(rpb, ((0, 0),) * 3 + ((GRID_W - half, lanes - n_dc - GRID_W + half),))
    geometry = [_na_window_geometry(t) for t in (0, 1, GRID_ROWS // NA_QROWS - 1)]
    ok_c = geometry[0][3]
    row_offset = [np.where(g[2], g[0], n_dr).tolist() for g in geometry]
    ok = jnp.asarray(np.concatenate([ok_c, ok_c], axis=1).astype(np.float32))
    src_spec = pl.BlockSpec((None, None, n_dr + 1, lanes), lambda l, h: (l, h, 0, 0))
    return _call(
        functools.partial(_na_bias_kernel, row_offset=row_offset), "na_bias",
        out_shape=jax.ShapeDtypeStruct((DEPTH, 3, 4, TQ_NA, TK_NA), F32),
        grid=(DEPTH, 4),
        in_specs=[src_spec, src_spec, pl.BlockSpec((GRID_W, lanes), lambda l, h: (0, 0))],
        out_specs=pl.BlockSpec((None, 3, None, TQ_NA, TK_NA), lambda l, h: (l, 0, h, 0, 0)),
    )(lo, hi, ok)


def _na_attention(qkv, qkv_c, bias, layer):
    tq = NA_SUB * TQ_NA
    nt = SEQ // tq
    kv = lambda col: pl.BlockSpec((SEQ, HEAD_DIM), lambda b, h, t: (b, col + h))
    kvc = lambda col: pl.BlockSpec((CTX_LEN, HEAD_DIM), lambda b, h, t: (b, col + h))
    bias_block = (None, 1, 1, TQ_NA, TK_NA)
    return _call(
        _na_kernel, "na_attention",
        out_shape=jax.ShapeDtypeStruct((BATCH * SEQ, 4 * HEAD_DIM), BF16),
        grid=(BATCH, 4, nt),
        in_specs=[
            pl.BlockSpec((tq, HEAD_DIM), lambda b, h, t: (b * nt + t, NA_Q + h)),
            kv(NA_K), kv(NA_V), kvc(NA_K), kvc(NA_V),
            pl.BlockSpec(bias_block, lambda b, h, t: (layer, jnp.where(t == 0, 0, 1), h, 0, 0)),
            pl.BlockSpec(bias_block, lambda b, h, t: (layer, 1, h, 0, 0)),
            pl.BlockSpec(bias_block, lambda b, h, t: (layer, jnp.where(t == nt - 1, 2, 1), h, 0, 0)),
        ],
        out_specs=pl.BlockSpec((tq, HEAD_DIM), lambda b, h, t: (b * nt + t, h)),
    )(qkv, qkv, qkv, qkv_c, qkv_c, bias, bias, bias)


def _gq_kernel(q_ref, kl_ref, vl_ref, kc_ref, vc_ref, o_ref, vaug):
    @pl.when(pl.program_id(2) == 0)
    def _():
        _fill_value_ones(vaug, vc_ref, vl_ref)

    q = jnp.concatenate([q_ref[:, 0:HEAD_DIM], q_ref[:, HEAD_DIM:]], axis=0)
    acc = _online_softmax_pv(q, *_flash_chunks(kl_ref, kc_ref, vaug))
    o = acc[:, 0:HEAD_DIM] / acc[:, HEAD_DIM:]
    o_ref[:, 0:HEAD_DIM] = o[0:TQ_FLASH].astype(BF16)
    o_ref[:, HEAD_DIM:] = o[TQ_FLASH:].astype(BF16)


def _gq_attention(qkv, qkv_c):
    nt = SEQ // TQ_FLASH
    kv = lambda col: pl.BlockSpec((SEQ, HEAD_DIM), lambda b, k, t: (b, col + k))
    kvc = lambda col: pl.BlockSpec((CTX_LEN, HEAD_DIM), lambda b, k, t: (b, col + k))
    return _call(
        _gq_kernel, "gq_attention",
        out_shape=jax.ShapeDtypeStruct((BATCH * SEQ, 4 * HEAD_DIM), BF16),
        grid=(BATCH, 2, nt),
        in_specs=[
            pl.BlockSpec((TQ_FLASH, 2 * HEAD_DIM), lambda b, k, t: (b * nt + t, GQ_Q // 2 + k)),
            kv(GQ_K), kv(GQ_V), kvc(GQ_K), kvc(GQ_V),
        ],
        out_specs=pl.BlockSpec((TQ_FLASH, 2 * HEAD_DIM), lambda b, k, t: (b * nt + t, k)),
        scratch_shapes=[pltpu.VMEM((CTX_LEN + SEQ, 2 * HEAD_DIM), BF16)],
    )(qkv, qkv, qkv, qkv_c, qkv_c)


def _df_kernel(lam_ref, subg_ref, q_ref, kl_ref, vl_ref, kc_ref, vc_ref, o_ref, vaug, *, lam_init):
    @pl.when(pl.program_id(2) == 0)
    def _():
        _fill_value_ones(vaug, vc_ref, vl_ref)

    lam = _diff_lambda(lam_ref, lam_init)
    acc = _online_softmax_pv(_split_diff_query(q_ref[...]), *_flash_chunks(kl_ref, kc_ref, vaug))
    o = acc[:, 0:HEAD_DIM] / acc[:, HEAD_DIM:]
    o = o[0:TQ_FLASH] - lam * o[TQ_FLASH:]
    o_ref[...] = (_rms(o) * (subg_ref[...] * (1.0 - lam_init))).astype(BF16)


def _df_attention(qkv, qkv_c, df_lambda, subg, lam_init):
    nt = SEQ // TQ_FLASH
    kv = lambda col: pl.BlockSpec((SEQ, HEAD_DIM), lambda b, h, t: (b, col + h))
    kvc = lambda col: pl.BlockSpec((CTX_LEN, HEAD_DIM), lambda b, h, t: (b, col + h))
    return _call(
        functools.partial(_df_kernel, lam_init=lam_init), "df_attention",
        out_shape=jax.ShapeDtypeStruct((BATCH * SEQ, 4 * HEAD_DIM), BF16),
        grid=(BATCH, 4, nt),
        in_specs=[
            pl.BlockSpec((4, DF_QK_DIM), lambda b, h, t: (0, 0)),
            pl.BlockSpec((1, HEAD_DIM), lambda b, h, t: (0, 0)),
            pl.BlockSpec((TQ_FLASH, HEAD_DIM), lambda b, h, t: (b * nt + t, DF_Q + h)),
            kv(DF_K), kv(DF_V), kvc(DF_K), kvc(DF_V),
        ],
        out_specs=pl.BlockSpec((TQ_FLASH, HEAD_DIM), lambda b, h, t: (b * nt + t, h)),
        scratch_shapes=[pltpu.VMEM((CTX_LEN + SEQ, 2 * HEAD_DIM), BF16)],
    )(df_lambda, subg, qkv, qkv, qkv, qkv_c, qkv_c)


def _sw_kernel(sink_ref, q_ref, kl_ref, vl_ref, kc_ref, vc_ref, m0_ref, m1_ref, m2_ref, o_ref):
    step = pl.program_id(2)
    mask_refs = [m0_ref] + [m1_ref] * (SW_SUB - 2) + [m2_ref]
    head = lax.broadcasted_iota(jnp.int32, (2 * TQ_SW, 1), 0) >= TQ_SW
    sink = jnp.where(head, sink_ref[0, 1:2, 0:1], sink_ref[0, 0:1, 0:1]) * LOG2E

    for j in range(SW_SUB):
        t = SW_SUB * step + j
        start = pl.multiple_of(jnp.clip(TQ_SW * t - SW_WINDOW, 0, SEQ - TK_SW), SW_WINDOW)
        keys = _rows(kl_ref[pl.ds(start, TK_SW), :], kc_ref[...])
        values = _rows(vl_ref[pl.ds(start, TK_SW), :], vc_ref[...])
        rows = slice(j * TQ_SW, (j + 1) * TQ_SW)
        q = _rows(q_ref[rows, 0:HEAD_DIM], q_ref[rows, HEAD_DIM:])
        o = _softmax_pv(_dot_nt(q, keys) + mask_refs[j][0], values, extra_logit=sink)
        o_ref[rows, 0:HEAD_DIM] = o[0:TQ_SW].astype(BF16)
        o_ref[rows, HEAD_DIM:] = o[TQ_SW:].astype(BF16)


def _sw_mask_table():
    r = np.arange(TQ_SW)[:, None]
    c = np.arange(TK_SW)[None, :]
    nt = SEQ // TQ_SW
    tables = []
    for t in (0, 1, nt - 1):
        start = min(max(TQ_SW * t - SW_WINDOW, 0), SEQ - TK_SW)
        ok = np.abs((TQ_SW * t + r) - (start + c)) <= SW_WINDOW
        band = np.where(ok, 0.0, NEG_INF).astype(np.float32)
        band = np.concatenate([band, np.zeros((TQ_SW, CTX_LEN), np.float32)], axis=1)
        tables.append(np.concatenate([band, band], axis=0))
    return jnp.asarray(np.stack(tables))


def _sw_attention(qkv, qkv_c, sink):
    tq = SW_SUB * TQ_SW
    nt = SEQ // tq
    mask = _sw_mask_table()
    kv = lambda col: pl.BlockSpec((SEQ, HEAD_DIM), lambda b, k, t: (b, col + k))
    kvc = lambda col: pl.BlockSpec((CTX_LEN, HEAD_DIM), lambda b, k, t: (b, col + k))
    mask_block = (1, 2 * TQ_SW, TK_SW + CTX_LEN)
    return _call(
        _sw_kernel, "sw_attention",
        out_shape=jax.ShapeDtypeStruct((BATCH * SEQ, 4 * HEAD_DIM), BF16),
        grid=(BATCH, 2, nt),
        in_specs=[
            pl.BlockSpec((1, 2, HEAD_DIM), lambda b, k, t: (k, 0, 0)),
            pl.BlockSpec((tq, 2 * HEAD_DIM), lambda b, k, t: (b * nt + t, SW_Q // 2 + k)),
            kv(SW_K), kv(SW_V), kvc(SW_K), kvc(SW_V),
            pl.BlockSpec(mask_block, lambda b, k, t: (jnp.where(t == 0, 0, 1), 0, 0)),
            pl.BlockSpec(mask_block, lambda b, k, t: (1, 0, 0)),
            pl.BlockSpec(mask_block, lambda b, k, t: (jnp.where(t == nt - 1, 2, 1), 0, 0)),
        ],
        out_specs=pl.BlockSpec((tq, 2 * HEAD_DIM), lambda b, k, t: (b * nt + t, k)),
    )(sink.reshape(2, 2, HEAD_DIM), qkv, qkv, qkv, qkv_c, qkv_c, mask, mask, mask)


def _ctx_attn_kernel(lam_ref, subg_ref, sink_ref, x_ref, o_ref, *, lam_init):
    chunk = lambda j: x_ref[:, j * HEAD_DIM:(j + 1) * HEAD_DIM]

    def put(j, o):
        o_ref[:, j * HEAD_DIM:(j + 1) * HEAD_DIM] = o.astype(BF16)

    lam = _diff_lambda(lam_ref, lam_init)
    for h in range(4):
        put(h, _softmax_pv(_dot_nt(chunk(NA_Q + h), chunk(NA_K + h)), chunk(NA_V + h)))
        put(4 + h, _softmax_pv(_dot_nt(chunk(GQ_Q + h), chunk(GQ_K + h // 2)),
                               chunk(GQ_V + h // 2)))
        q2 = _split_diff_query(chunk(DF_Q + h))
        o = _softmax_pv(_dot_nt(q2, chunk(DF_K + h)), chunk(DF_V + h))
        o = o[0:CTX_LEN] - lam * o[CTX_LEN:]
        put(8 + h, _rms(o) * (subg_ref[...] * (1.0 - lam_init)))
        put(12 + h, _softmax_pv(_dot_nt(chunk(SW_Q + h), chunk(SW_K + h // 2)),
                                chunk(SW_V + h // 2),
                                extra_logit=sink_ref[h:h + 1, 0:1] * LOG2E))


def _ctx_attention(qkv_c, df_lambda, subg, sink, lam_init):
    return _call(
        functools.partial(_ctx_attn_kernel, lam_init=lam_init), "ctx_attention",
        out_shape=jax.ShapeDtypeStruct((BATCH * CTX_LEN, D_MODEL), BF16),
        grid=(BATCH,),
        in_specs=[
            pl.BlockSpec((4, DF_QK_DIM), lambda b: (0, 0)),
            pl.BlockSpec((1, HEAD_DIM), lambda b: (0, 0)),
            pl.BlockSpec((4, HEAD_DIM), lambda b: (0, 0)),
            pl.BlockSpec((CTX_LEN, IN_WIDTH), lambda b: (b, 0)),
        ],
        out_specs=pl.BlockSpec((CTX_LEN, D_MODEL), lambda b: (b, 0)),
    )(df_lambda, subg, sink, qkv_c)


def _outproj_kernel(y0_ref, y1_ref, y2_ref, y3_ref, h_ref, mod_ref, g_ref, w32_ref, o_ref, f_ref,
                    w_ref):
    @pl.when(pl.program_id(0) == 0)
    def _():
        w_ref[...] = w32_ref[...].astype(BF16)

    gate = mod_ref[0, 2:3, :] * g_ref[1:2, :]
    ffn_gain = _gain(g_ref[2:3, :], mod_ref[0, 4:5, :])
    for c in range(TM_PROJ // ROW_SUB):
        rows = slice(c * ROW_SUB, (c + 1) * ROW_SUB)
        y = jnp.concatenate([y_ref[rows, :] for y_ref in (y0_ref, y1_ref, y2_ref, y3_ref)], axis=1)
        h_new = h_ref[rows, :] + _rms(_dot(y, w_ref[...])) * gate
        o_ref[rows, :] = h_new
        f_ref[rows, :] = (_rms(h_new) * ffn_gain + mod_ref[0, 3:4, :]).astype(BF16)


def _out_projection(ys, y_cols, h, mod, mod_row, g, w_out, layer):
    rows = h.shape[0]
    tm = TM_PROJ
    y_specs = [pl.BlockSpec((tm, MIX_COLS), lambda i, c=c: (i, c)) for c in y_cols]
    row_block = pl.BlockSpec((tm, D_MODEL), lambda i: (i, 0))
    return _call(
        _outproj_kernel, "out_projection",
        out_shape=(jax.ShapeDtypeStruct((rows, D_MODEL), F32),
                   jax.ShapeDtypeStruct((rows, D_MODEL), BF16)),
        grid=(rows // tm,),
        in_specs=y_specs + [
            row_block,
            pl.BlockSpec((1, 6, D_MODEL), lambda i: (mod_row(i), 0, 0)),
            pl.BlockSpec((4, D_MODEL), lambda i: (0, 0)),
            _resident_layer(layer, (D_MODEL, D_MODEL)),
        ],
        out_specs=(row_block, row_block),
        scratch_shapes=[pltpu.VMEM((D_MODEL, D_MODEL), BF16)],
    )(*ys, h, mod, g, w_out)


def _ffn_up_kernel(f_ref, wg_ref, wu_ref, o_ref, wg_scr, wu_scr):
    @pl.when(pl.program_id(1) == 0)
    def _():
        wg_scr[...] = wg_ref[...].astype(BF16)
        wu_scr[...] = wu_ref[...].astype(BF16)

    f = f_ref[...]
    gate = _dot(f, wg_scr[...])
    up = _dot(f, wu_scr[...])
    o_ref[...] = (gate * jax.nn.sigmoid(gate) * up).astype(BF16)


def _ffn_up(f, w_gate_up, layer):
    rows = f.shape[0]
    tm, tf = min(TM_UP, rows), TF_UP
    nf = D_FF // tf
    return _call(
        _ffn_up_kernel, "ffn_up",
        out_shape=jax.ShapeDtypeStruct((rows, D_FF), BF16),
        grid=(nf, rows // tm),
        in_specs=[
            pl.BlockSpec((tm, D_MODEL), lambda k, i: (i, 0)),
            pl.BlockSpec((None, D_MODEL, tf), lambda k, i: (layer, 0, k)),
            pl.BlockSpec((None, D_MODEL, tf), lambda k, i: (layer, 0, nf + k)),
        ],
        out_specs=pl.BlockSpec((tm, tf), lambda k, i: (i, k)),
        scratch_shapes=[pltpu.VMEM((D_MODEL, tf), BF16), pltpu.VMEM((D_MODEL, tf), BF16)],
    )(f, w_gate_up, w_gate_up)


def _ffn_down_kernel(a_ref, h_ref, mod_ref, g_ref, w_ref, o_ref):
    gate = mod_ref[0, 5:6, :] * g_ref[3:4, :]
    for c in range(TM_PROJ // ROW_SUB):
        rows = slice(c * ROW_SUB, (c + 1) * ROW_SUB)
        acc = _dot(a_ref[rows, :], w_ref[...])
        o_ref[rows, :] = h_ref[rows, :] + _rms(acc) * gate


def _ffn_down(act, h, mod, mod_row, g, w_down, layer):
    rows = h.shape[0]
    tm = TM_PROJ
    row_block = pl.BlockSpec((tm, D_MODEL), lambda i: (i, 0))
    return _call(
        _ffn_down_kernel, "ffn_down",
        out_shape=jax.ShapeDtypeStruct((rows, D_MODEL), F32),
        grid=(rows // tm,),
        in_specs=[
            pl.BlockSpec((tm, D_FF), lambda i: (i, 0)),
            row_block,
            pl.BlockSpec((1, 6, D_MODEL), lambda i: (mod_row(i), 0, 0)),
            pl.BlockSpec((4, D_MODEL), lambda i: (0, 0)),
            _resident_layer(layer, (D_FF, D_MODEL)),
        ],
        out_specs=row_block,
    )(act, h, mod, g, w_down)


def _rope_table(dim):
    q = dim // 4
    t = jnp.arange(SEQ)
    pos = jnp.stack([t // GRID_W, t % GRID_W], axis=-1).astype(F32)
    inv = ROPE_THETA ** (-jnp.arange(q, dtype=F32) / q)
    ang = pos[:, :, None] * inv
    cos = jnp.broadcast_to(jnp.cos(ang)[:, :, None, :], (SEQ, 2, 2, q)).reshape(SEQ, dim)
    sin = jnp.sin(ang)
    sin = jnp.stack([-sin, sin], axis=2).reshape(SEQ, dim)
    reps = HEAD_DIM // dim
    return jnp.tile(cos, (1, reps)), jnp.tile(sin, (1, reps))


def kernel(x, c, ctx, c_ctx, w_mod, b_mod, norm_g, w_in, w_out, na_rpb, qk_norm_g,
           df_lambda, df_subln_g, sw_sink, w_gate_up, w_down):
    pad_rows = jnp.zeros((MOD_ROWS - 1 - BATCH, D_MODEL), F32)
    cvec = jnp.concatenate([c_ctx[None, :], c, pad_rows], axis=0)
    mod_all = _modulation(cvec, w_mod, b_mod).reshape(DEPTH, MOD_ROWS, 6, D_MODEL)
    rope_tab = jnp.stack(_rope_table(HEAD_DIM) + _rope_table(DF_QK_DIM))
    na_bias = _na_bias_table(na_rpb)

    tiles_per_seq = SEQ // TM_PROJ
    lat_row = lambda i: 1 + i // tiles_per_seq
    ctx_row = lambda i: 0

    w_in, w_down = w_in.astype(BF16), w_down.astype(BF16)

    h = x.reshape(BATCH * SEQ, D_MODEL)
    hc = ctx.reshape(BATCH * CTX_LEN, D_MODEL)
    for l in range(DEPTH):
        ctx_out = l < DEPTH - 1
        lam_init = 0.8 - 0.6 * math.exp(-0.3 * l)
        mod = mod_all[l]
        g = norm_g[l]
        subg = df_subln_g[l][None, :]
        sink = jnp.broadcast_to(sw_sink[l][:, None], (4, HEAD_DIM))

        qkv = _in_projection(h, mod, lat_row, g, qk_norm_g[l], w_in, l, rope_tab)
        qkv_c = _in_projection(hc, mod, ctx_row, g, qk_norm_g[l], w_in, l, None)

        ys = [
            _na_attention(qkv, qkv_c, na_bias, l),
            _gq_attention(qkv, qkv_c),
            _df_attention(qkv, qkv_c, df_lambda[l], subg, lam_init),
            _sw_attention(qkv, qkv_c, sink),
        ]
        h, f = _out_projection(ys, (0, 0, 0, 0), h, mod, lat_row, g, w_out, l)
        h = _ffn_down(_ffn_up(f, w_gate_up, l), h, mod, lat_row, g, w_down, l)
        if ctx_out:
            yc = _ctx_attention(qkv_c, df_lambda[l], subg, sink, lam_init)
            hc, fc = _out_projection([yc] * 4, (0, 1, 2, 3), hc, mod, ctx_row, g, w_out, l)
            hc = _ffn_down(_ffn_up(fc, w_gate_up, l), hc, mod, ctx_row, g, w_down, l)
    return h.reshape(BATCH, SEQ, D_MODEL)
```

```python
import functools
import math

import jax
import jax.numpy as jnp
import numpy as np
from jax import lax
from jax.experimental import pallas as pl
from jax.experimental.pallas import tpu as pltpu

D_MODEL = 2048
BATCH = 4
SEQ = 4096
DEPTH = 2
GRID_W = 64
GRID_ROWS = SEQ // GRID_W
CTX_LEN = 256
HEAD_DIM = 128
NA_KH = 8
NA_KW = 16
DF_QK_DIM = 64
SW_WINDOW = 128
D_FF = 5632
IN_WIDTH = 5120
ROPE_THETA = 10000.0
EPS = 1e-6
NEG_INF = -1e30
LOG2E = math.log2(math.e)

NA_Q, NA_K, NA_V = 0, 4, 8
GQ_Q, GQ_K, GQ_V = 12, 16, 18
DF_Q, DF_K, DF_V = 20, 24, 28
SW_Q, SW_K, SW_V = 32, 36, 38

F32 = jnp.float32
BF16 = jnp.bfloat16
MIB = 1024 * 1024

TM_PROJ = 512
ROW_SUB = 256
TM_UP = 1024
TF_UP = 512
TN_MOD = 1536
TQ_FLASH = 512
FLASH_CHUNKS = (2048, 2048)
assert sum(FLASH_CHUNKS) == SEQ
NA_QROWS = 4
NA_KROWS = 12
TQ_NA = NA_QROWS * GRID_W
TK_NA = NA_KROWS * GRID_W
TQ_SW = 256
TK_SW = TQ_SW + 2 * SW_WINDOW
NA_SUB = 16
SW_SUB = 4
MOD_ROWS = 8
MIX_COLS = 4 * HEAD_DIM
ROT_HALF = HEAD_DIM // 4
ROT_HALF_DF = DF_QK_DIM // 4

VMEM_LIMIT_MIB = {
    "modulation": 40, "in_projection": 56, "na_bias": 32, "na_attention": 40,
    "gq_attention": 48, "df_attention": 48, "sw_attention": 40, "ctx_attention": 32,
    "out_projection": 56, "ffn_up": 48, "ffn_down": 58,
}


def _call(kernel, name, **kwargs):
    params = pltpu.CompilerParams(vmem_limit_bytes=VMEM_LIMIT_MIB[name] * MIB)
    return pl.pallas_call(kernel, name=name, compiler_params=params, **kwargs)


def _resident_layer(layer, shape):
    index = (layer,) + (0,) * len(shape)
    return pl.BlockSpec((None,) + tuple(shape), lambda *_: index, pipeline_mode=pl.Buffered(1))


def _resident(shape):
    index = (0,) * len(shape)
    return pl.BlockSpec(tuple(shape), lambda *_: index, pipeline_mode=pl.Buffered(1))


def _gain(g, scale=None):
    return g if scale is None else g * (1.0 + scale)


def _rms(x):
    return x * lax.rsqrt(jnp.mean(x * x, axis=-1, keepdims=True) + EPS)


def _dot(a, b):
    return jnp.dot(a, b, preferred_element_type=F32)


def _dot_nt(a, b):
    return lax.dot_general(a, b, (((1,), (1,)), ((), ())), preferred_element_type=F32)


def _mod_kernel(c_ref, w_ref, b_ref, o_ref):
    c = c_ref[...]
    s = c * jax.nn.sigmoid(c)
    hi = s.astype(BF16)
    lo = (s - hi.astype(F32)).astype(BF16)
    r = _dot(jnp.concatenate([hi, lo], axis=0), w_ref[0].astype(BF16))
    o_ref[0] = r[0:MOD_ROWS] + r[MOD_ROWS:] + b_ref[0]


def _modulation(cvec, w_mod, b_mod):
    n = 6 * D_MODEL
    return _call(
        _mod_kernel, "modulation",
        out_shape=jax.ShapeDtypeStruct((DEPTH, MOD_ROWS, n), F32),
        grid=(DEPTH, n // TN_MOD),
        in_specs=[
            pl.BlockSpec((MOD_ROWS, D_MODEL), lambda l, j: (0, 0)),
            pl.BlockSpec((1, D_MODEL, TN_MOD), lambda l, j: (l, 0, j)),
            pl.BlockSpec((1, 1, TN_MOD), lambda l, j: (l, 0, j)),
        ],
        out_specs=pl.BlockSpec((1, MOD_ROWS, TN_MOD), lambda l, j: (l, 0, j)),
    )(cvec, w_mod, b_mod.reshape(DEPTH, 1, n))


def _swap_halves(x, half):
    lane = lax.broadcasted_iota(jnp.int32, x.shape, 1)
    fwd = pltpu.roll(x, HEAD_DIM - half, axis=1)
    bwd = pltpu.roll(x, half, axis=1)
    return jnp.where((lane & (2 * half - 1)) < half, fwd, bwd)


def _inproj_kernel(*refs, rope):
    if rope:
        h_ref, mod_ref, g_ref, qkg_ref, w_ref, rope_ref, o_ref, a_scr = refs
    else:
        h_ref, mod_ref, g_ref, qkg_ref, w_ref, o_ref, a_scr = refs
        rope_ref = None
    a = _rms(h_ref[...]) * _gain(g_ref[0:1, :], mod_ref[0, 1:2, :]) + mod_ref[0, 0:1, :]
    a_scr[...] = a.astype(BF16)

    def rot(x, table, half):
        if rope_ref is None:
            return x
        return x * rope_ref[table] + _swap_halves(x, half) * rope_ref[table + 1]

    scale = HEAD_DIM ** -0.5 * LOG2E
    scale_df = DF_QK_DIM ** -0.5 * LOG2E
    plain = lambda x: x
    groups = [
        (NA_Q, 4, lambda x: x * scale),
        (NA_K, 4, plain),
        (NA_V, 4, plain),
        (GQ_Q, 4, lambda x: rot(_rms(x) * qkg_ref[0:1, :], 0, ROT_HALF) * scale),
        (GQ_K, 2, lambda x: rot(_rms(x) * qkg_ref[1:2, :], 0, ROT_HALF)),
        (GQ_V, 2, plain),
        (DF_Q, 4, lambda x: rot(x, 2, ROT_HALF_DF) * scale_df),
        (DF_K, 4, lambda x: rot(x, 2, ROT_HALF_DF)),
        (DF_V, 4, plain),
        (SW_Q, 4, lambda x: rot(x, 0, ROT_HALF) * scale),
        (SW_K, 2, lambda x: rot(x, 0, ROT_HALF)),
        (SW_V, 2, plain),
    ]
    for first, count, fn in groups:
        c0 = first * HEAD_DIM
        acc = _dot(a_scr[...], w_ref[:, c0:c0 + count * HEAD_DIM])
        for j in range(count):
            x = fn(acc[:, j * HEAD_DIM:(j + 1) * HEAD_DIM])
            o_ref[:, c0 + j * HEAD_DIM:c0 + (j + 1) * HEAD_DIM] = x.astype(BF16)


def _in_projection(h, mod, mod_row, g, qkg, w_in, rope_tab):
    rows = h.shape[0]
    tm = TM_PROJ
    rope = rope_tab is not None
    in_specs = [
        pl.BlockSpec((tm, D_MODEL), lambda i: (i, 0)),
        pl.BlockSpec((1, 6, D_MODEL), lambda i: (mod_row(i), 0, 0)),
        pl.BlockSpec((4, D_MODEL), lambda i: (0, 0)),
        pl.BlockSpec((2, HEAD_DIM), lambda i: (0, 0)),
        _resident((D_MODEL, IN_WIDTH)),
    ]
    args = [h, mod, g, qkg, w_in]
    if rope:
        tiles_per_seq = SEQ // tm
        in_specs.append(pl.BlockSpec((4, tm, HEAD_DIM), lambda i: (0, i % tiles_per_seq, 0)))
        args.append(rope_tab)
    return _call(
        functools.partial(_inproj_kernel, rope=rope), "in_projection",
        out_shape=jax.ShapeDtypeStruct((rows, IN_WIDTH), BF16),
        grid=(rows // tm,),
        in_specs=in_specs,
        out_specs=pl.BlockSpec((tm, IN_WIDTH), lambda i: (i, 0)),
        scratch_shapes=[pltpu.VMEM((tm, D_MODEL), BF16)],
    )(*args)


def _softmax_pv(s, v, extra_logit=None):
    m = jnp.max(s, axis=-1, keepdims=True)
    if extra_logit is not None:
        m = jnp.maximum(m, extra_logit)
    p = jnp.exp2(s - m)
    l = jnp.sum(p, axis=-1, keepdims=True)
    if extra_logit is not None:
        l = l + jnp.exp2(extra_logit - m)
    return _dot(p.astype(BF16), v) * (1.0 / l)


def _softmax_pv_blocks(scores, values):
    m = functools.reduce(jnp.maximum, [jnp.max(s, axis=-1, keepdims=True) for s in scores])
    l = None
    o = None
    for s, v in zip(scores, values):
        p = jnp.exp2(s - m)
        li = jnp.sum(p, axis=-1, keepdims=True)
        oi = _dot(p.astype(BF16), v)
        l = li if l is None else l + li
        o = oi if o is None else o + oi
    return o * (1.0 / l)


def _rows(*blocks):
    return jnp.concatenate(blocks, axis=0)


def _online_softmax_pv(q, key_chunks, value_chunks):
    n = len(key_chunks)
    s_next = _dot_nt(q, key_chunks[0]())
    m = None
    acc = None
    for i in range(n):
        s = s_next
        if i + 1 < n:
            s_next = _dot_nt(q, key_chunks[i + 1]())
        mi = jnp.max(s, axis=-1, keepdims=True)
        m_new = mi if m is None else jnp.maximum(m, mi)
        pv = _dot(jnp.exp2(s - m_new).astype(BF16), value_chunks[i]())
        acc = pv if acc is None else acc * jnp.exp2(m - m_new) + pv
        m = m_new
    return acc


def _fill_value_ones(vaug, vc_ref, vl_ref):
    vaug[0:CTX_LEN, 0:HEAD_DIM] = vc_ref[...]
    vaug[CTX_LEN:, 0:HEAD_DIM] = vl_ref[...]
    vaug[:, HEAD_DIM:] = jnp.ones((CTX_LEN + SEQ, HEAD_DIM), BF16)


def _flash_chunks(kl_ref, kc_ref, vaug):
    keys, values = [], []
    start = 0
    for size in FLASH_CHUNKS:
        keys.append(lambda a=start, b=start + size: kl_ref[a:b, :])
        values.append(lambda a=start, b=start + size: vaug[CTX_LEN + a:CTX_LEN + b, :])
        start += size
    keys.append(lambda: kc_ref[...])
    values.append(lambda: vaug[0:CTX_LEN, :])
    return keys, values


def _diff_lambda(lam_ref, lam_init):
    lam = lam_ref[...]
    a = jnp.sum(lam[0:1, :] * lam[1:2, :], axis=-1, keepdims=True)
    b = jnp.sum(lam[2:3, :] * lam[3:4, :], axis=-1, keepdims=True)
    return jnp.exp(a) - jnp.exp(b) + lam_init


def _split_diff_query(q):
    lane = lax.broadcasted_iota(jnp.int32, q.shape, 1)
    zero = jnp.zeros_like(q)
    return jnp.concatenate([jnp.where(lane < DF_QK_DIM, q, zero),
                            jnp.where(lane >= DF_QK_DIM, q, zero)], axis=0)


def _na_kernel(q_ref, kl_ref, vl_ref, kc_ref, vc_ref, b0_ref, b1_ref, b2_ref, o_ref):
    step = pl.program_id(2)
    bias_refs = [b0_ref] + [b1_ref] * (NA_SUB - 2) + [b2_ref]

    def scores(j):
        t = NA_SUB * step + j
        first_row = jnp.clip(NA_QROWS * t - NA_KH // 2, 0, GRID_ROWS - NA_KROWS)
        start = pl.multiple_of(first_row * GRID_W, TQ_NA)
        q = q_ref[j * TQ_NA:(j + 1) * TQ_NA, :]
        s_loc = _dot_nt(q, kl_ref[pl.ds(start, TK_NA), :]) + bias_refs[j][0, 0]
        return start, [s_loc, _dot_nt(q, kc_ref[...])]

    nxt = scores(0)
    for j in range(NA_SUB):
        start, s = nxt
        if j + 1 < NA_SUB:
            nxt = scores(j + 1)
        o = _softmax_pv_blocks(s, [vl_ref[pl.ds(start, TK_NA), :], vc_ref[...]])
        o_ref[j * TQ_NA:(j + 1) * TQ_NA, :] = o.astype(BF16)


def _na_window_geometry(t):
    first_row = min(max(NA_QROWS * t - NA_KH // 2, 0), GRID_ROWS - NA_KROWS)
    r = NA_QROWS * t + np.arange(NA_QROWS)
    kr = first_row + np.arange(NA_KROWS)
    c = np.arange(GRID_W)
    kc = np.arange(GRID_W)
    r0 = np.clip(r - NA_KH // 2, 0, GRID_ROWS - NA_KH)
    c0 = np.clip(c - NA_KW // 2, 0, GRID_W - NA_KW)
    ok_r = (kr[None, :] >= r0[:, None]) & (kr[None, :] < r0[:, None] + NA_KH)
    ok_c = (kc[None, :] >= c0[:, None]) & (kc[None, :] < c0[:, None] + NA_KW)
    dr = np.clip(kr[None, :] - r[:, None] + NA_KH - 1, 0, 2 * NA_KH - 2)
    dc = np.clip(kc[None, :] - c[:, None] + NA_KW - 1, 0, 2 * NA_KW - 2)
    return dr, dc, ok_r, ok_c


def _na_bias_kernel(lo_ref, hi_ref, ok_ref, o_ref, *, row_offset):
    ok = ok_ref[...] > 0.0
    for v in range(3):
        for r in range(NA_QROWS):
            for pair in range(NA_KROWS // 2):
                d0 = row_offset[v][r][2 * pair]
                d1 = row_offset[v][r][2 * pair + 1]
                src = lo_ref[d0:d0 + 1, :] + hi_ref[d1:d1 + 1, :]
                block = pltpu.roll(jnp.broadcast_to(src, (GRID_W, 2 * GRID_W)), 0, 1,
                                   stride=1, stride_axis=0)
                o_ref[v, r * GRID_W:(r + 1) * GRID_W, pair * 2 * GRID_W:(pair + 1) * 2 * GRID_W] = (
                    jnp.where(ok, block, NEG_INF))


def _na_bias_table(na_rpb):
    n_dr, n_dc = 2 * NA_KH - 1, 2 * NA_KW - 1
    half = NA_KW - 1
    rpb = jnp.pad(na_rpb.astype(F32) * LOG2E, ((0, 0), (0, 0), (0, 1), (0, 0)),
                  constant_values=NEG_INF)
    lanes = 2 * GRID_W
    lo = jnp.roll(jnp.pad(rpb, ((0, 0),) * 3 + ((0, lanes - n_dc),)), -half, axis=-1)
    hi = jnp.pad(rpb, ((0, 0),) * 3 + ((GRID_W - half, lanes - n_dc - GRID_W + half),))
    geometry = [_na_window_geometry(t) for t in (0, 1, GRID_ROWS // NA_QROWS - 1)]
    ok_c = geometry[0][3]
    row_offset = [np.where(g[2], g[0], n_dr).tolist() for g in geometry]
    ok = jnp.asarray(np.concatenate([ok_c, ok_c], axis=1).astype(np.float32))
    src_spec = pl.BlockSpec((None, None, n_dr + 1, lanes), lambda l, h: (l, h, 0, 0))
    return _call(
        functools.partial(_na_bias_kernel, row_offset=row_offset), "na_bias",
        out_shape=jax.ShapeDtypeStruct((DEPTH, 3, 4, TQ_NA, TK_NA), F32),
        grid=(DEPTH, 4),
        in_specs=[src_spec, src_spec, pl.BlockSpec((GRID_W, lanes), lambda l, h: (0, 0))],
        out_specs=pl.BlockSpec((None, 3, None, TQ_NA, TK_NA), lambda l, h: (l, 0, h, 0, 0)),
    )(lo, hi, ok)


def _na_attention(qkv, qkv_c, bias, layer):
    tq = NA_SUB * TQ_NA
    nt = SEQ // tq
    kv = lambda col: pl.BlockSpec((SEQ, HEAD_DIM), lambda b, h, t: (b, col + h))
    kvc = lambda col: pl.BlockSpec((CTX_LEN, HEAD_DIM), lambda b, h, t: (b, col + h))
    bias_block = (None, 1, 1, TQ_NA, TK_NA)
    return _call(
        _na_kernel, "na_attention",
        out_shape=jax.ShapeDtypeStruct((BATCH * SEQ, 4 * HEAD_DIM), BF16),
        grid=(BATCH, 4, nt),
        in_specs=[
            pl.BlockSpec((tq, HEAD_DIM), lambda b, h, t: (b * nt + t, NA_Q + h)),
            kv(NA_K), kv(NA_V), kvc(NA_K), kvc(NA_V),
            pl.BlockSpec(bias_block, lambda b, h, t: (layer, jnp.where(t == 0, 0, 1), h, 0, 0)),
            pl.BlockSpec(bias_block, lambda b, h, t: (layer, 1, h, 0, 0)),
            pl.BlockSpec(bias_block, lambda b, h, t: (layer, jnp.where(t == nt - 1, 2, 1), h, 0, 0)),
        ],
        out_specs=pl.BlockSpec((tq, HEAD_DIM), lambda b, h, t: (b * nt + t, h)),
    )(qkv, qkv, qkv, qkv_c, qkv_c, bias, bias, bias)


def _gq_kernel(q_ref, kl_ref, vl_ref, kc_ref, vc_ref, o_ref, vaug):
    @pl.when(pl.program_id(2) == 0)
    def _():
        _fill_value_ones(vaug, vc_ref, vl_ref)

    q = jnp.concatenate([q_ref[:, 0:HEAD_DIM], q_ref[:, HEAD_DIM:]], axis=0)
    acc = _online_softmax_pv(q, *_flash_chunks(kl_ref, kc_ref, vaug))
    o = acc[:, 0:HEAD_DIM] / acc[:, HEAD_DIM:]
    o_ref[:, 0:HEAD_DIM] = o[0:TQ_FLASH].astype(BF16)
    o_ref[:, HEAD_DIM:] = o[TQ_FLASH:].astype(BF16)


def _gq_attention(qkv, qkv_c):
    nt = SEQ // TQ_FLASH
    kv = lambda col: pl.BlockSpec((SEQ, HEAD_DIM), lambda b, k, t: (b, col + k))
    kvc = lambda col: pl.BlockSpec((CTX_LEN, HEAD_DIM), lambda b, k, t: (b, col + k))
    return _call(
        _gq_kernel, "gq_attention",
        out_shape=jax.ShapeDtypeStruct((BATCH * SEQ, 4 * HEAD_DIM), BF16),
        grid=(BATCH, 2, nt),
        in_specs=[
            pl.BlockSpec((TQ_FLASH, 2 * HEAD_DIM), lambda b, k, t: (b * nt + t, GQ_Q // 2 + k)),
            kv(GQ_K), kv(GQ_V), kvc(GQ_K), kvc(GQ_V),
        ],
        out_specs=pl.BlockSpec((TQ_FLASH, 2 * HEAD_DIM), lambda b, k, t: (b * nt + t, k)),
        scratch_shapes=[pltpu.VMEM((CTX_LEN + SEQ, 2 * HEAD_DIM), BF16)],
    )(qkv, qkv, qkv, qkv_c, qkv_c)


def _df_kernel(lam_ref, subg_ref, q_ref, kl_ref, vl_ref, kc_ref, vc_ref, o_ref, vaug, *, lam_init):
    @pl.when(pl.program_id(2) == 0)
    def _():
        _fill_value_ones(vaug, vc_ref, vl_ref)

    lam = _diff_lambda(lam_ref, lam_init)
    acc = _online_softmax_pv(_split_diff_query(q_ref[...]), *_flash_chunks(kl_ref, kc_ref, vaug))
    o = acc[:, 0:HEAD_DIM] / acc[:, HEAD_DIM:]
    o = o[0:TQ_FLASH] - lam * o[TQ_FLASH:]
    o_ref[...] = (_rms(o) * (subg_ref[...] * (1.0 - lam_init))).astype(BF16)


def _df_attention(qkv, qkv_c, df_lambda, subg, lam_init):
    nt = SEQ // TQ_FLASH
    kv = lambda col: pl.BlockSpec((SEQ, HEAD_DIM), lambda b, h, t: (b, col + h))
    kvc = lambda col: pl.BlockSpec((CTX_LEN, HEAD_DIM), lambda b, h, t: (b, col + h))
    return _call(
        functools.partial(_df_kernel, lam_init=lam_init), "df_attention",
        out_shape=jax.ShapeDtypeStruct((BATCH * SEQ, 4 * HEAD_DIM), BF16),
        grid=(BATCH, 4, nt),
        in_specs=[
            pl.BlockSpec((4, DF_QK_DIM), lambda b, h, t: (0, 0)),
            pl.BlockSpec((1, HEAD_DIM), lambda b, h, t: (0, 0)),
            pl.BlockSpec((TQ_FLASH, HEAD_DIM), lambda b, h, t: (b * nt + t, DF_Q + h)),
            kv(DF_K), kv(DF_V), kvc(DF_K), kvc(DF_V),
        ],
        out_specs=pl.BlockSpec((TQ_FLASH, HEAD_DIM), lambda b, h, t: (b * nt + t, h)),
        scratch_shapes=[pltpu.VMEM((CTX_LEN + SEQ, 2 * HEAD_DIM), BF16)],
    )(df_lambda, subg, qkv, qkv, qkv, qkv_c, qkv_c)


def _sw_kernel(sink_ref, q_ref, kl_ref, vl_ref, kc_ref, vc_ref, m0_ref, m1_ref, m2_ref, o_ref):
    step = pl.program_id(2)
    mask_refs = [m0_ref] + [m1_ref] * (SW_SUB - 2) + [m2_ref]
    head = lax.broadcasted_iota(jnp.int32, (2 * TQ_SW, 1), 0) >= TQ_SW
    sink = jnp.where(head, sink_ref[0, 1:2, 0:1], sink_ref[0, 0:1, 0:1]) * LOG2E

    for j in range(SW_SUB):
        t = SW_SUB * step + j
        start = pl.multiple_of(jnp.clip(TQ_SW * t - SW_WINDOW, 0, SEQ - TK_SW), SW_WINDOW)
        keys = _rows(kl_ref[pl.ds(start, TK_SW), :], kc_ref[...])
        values = _rows(vl_ref[pl.ds(start, TK_SW), :], vc_ref[...])
        rows = slice(j * TQ_SW, (j + 1) * TQ_SW)
        q = _rows(q_ref[rows, 0:HEAD_DIM], q_ref[rows, HEAD_DIM:])
        o = _softmax_pv(_dot_nt(q, keys) + mask_refs[j][0], values, extra_logit=sink)
        o_ref[rows, 0:HEAD_DIM] = o[0:TQ_SW].astype(BF16)
        o_ref[rows, HEAD_DIM:] = o[TQ_SW:].astype(BF16)


def _sw_mask_table():
    r = np.arange(TQ_SW)[:, None]
    c = np.arange(TK_SW)[None, :]
    nt = SEQ // TQ_SW
    tables = []
    for t in (0, 1, nt - 1):
        start = min(max(TQ_SW * t - SW_WINDOW, 0), SEQ - TK_SW)
        ok = np.abs((TQ_SW * t + r) - (start + c)) <= SW_WINDOW
        band = np.where(ok, 0.0, NEG_INF).astype(np.float32)
        band = np.concatenate([band, np.zeros((TQ_SW, CTX_LEN), np.float32)], axis=1)
        tables.append(np.concatenate([band, band], axis=0))
    return jnp.asarray(np.stack(tables))


def _sw_attention(qkv, qkv_c, sink):
    tq = SW_SUB * TQ_SW
    nt = SEQ // tq
    mask = _sw_mask_table()
    kv = lambda col: pl.BlockSpec((SEQ, HEAD_DIM), lambda b, k, t: (b, col + k))
    kvc = lambda col: pl.BlockSpec((CTX_LEN, HEAD_DIM), lambda b, k, t: (b, col + k))
    mask_block = (1, 2 * TQ_SW, TK_SW + CTX_LEN)
    return _call(
        _sw_kernel, "sw_attention",
        out_shape=jax.ShapeDtypeStruct((BATCH * SEQ, 4 * HEAD_DIM), BF16),
        grid=(BATCH, 2, nt),
        in_specs=[
            pl.BlockSpec((1, 2, HEAD_DIM), lambda b, k, t: (k, 0, 0)),
            pl.BlockSpec((tq, 2 * HEAD_DIM), lambda b, k, t: (b * nt + t, SW_Q // 2 + k)),
            kv(SW_K), kv(SW_V), kvc(SW_K), kvc(SW_V),
            pl.BlockSpec(mask_block, lambda b, k, t: (jnp.where(t == 0, 0, 1), 0, 0)),
            pl.BlockSpec(mask_block, lambda b, k, t: (1, 0, 0)),
            pl.BlockSpec(mask_block, lambda b, k, t: (jnp.where(t == nt - 1, 2, 1), 0, 0)),
        ],
        out_specs=pl.BlockSpec((tq, 2 * HEAD_DIM), lambda b, k, t: (b * nt + t, k)),
    )(sink.reshape(2, 2, HEAD_DIM), qkv, qkv, qkv, qkv_c, qkv_c, mask, mask, mask)


def _ctx_attn_kernel(lam_ref, subg_ref, sink_ref, x_ref, o_ref, *, lam_init):
    chunk = lambda j: x_ref[:, j * HEAD_DIM:(j + 1) * HEAD_DIM]

    def put(j, o):
        o_ref[:, j * HEAD_DIM:(j + 1) * HEAD_DIM] = o.astype(BF16)

    lam = _diff_lambda(lam_ref, lam_init)
    for h in range(4):
        put(h, _softmax_pv(_dot_nt(chunk(NA_Q + h), chunk(NA_K + h)), chunk(NA_V + h)))
        put(4 + h, _softmax_pv(_dot_nt(chunk(GQ_Q + h), chunk(GQ_K + h // 2)),
                               chunk(GQ_V + h // 2)))
        q2 = _split_diff_query(chunk(DF_Q + h))
        o = _softmax_pv(_dot_nt(q2, chunk(DF_K + h)), chunk(DF_V + h))
        o = o[0:CTX_LEN] - lam * o[CTX_LEN:]
        put(8 + h, _rms(o) * (subg_ref[...] * (1.0 - lam_init)))
        put(12 + h, _softmax_pv(_dot_nt(chunk(SW_Q + h), chunk(SW_K + h // 2)),
                                chunk(SW_V + h // 2),
                                extra_logit=sink_ref[h:h + 1, 0:1] * LOG2E))


def _ctx_attention(qkv_c, df_lambda, subg, sink, lam_init):
    return _call(
        functools.partial(_ctx_attn_kernel, lam_init=lam_init), "ctx_attention",
        out_shape=jax.ShapeDtypeStruct((BATCH * CTX_LEN, D_MODEL), BF16),
        grid=(BATCH,),
        in_specs=[
            pl.BlockSpec((4, DF_QK_DIM), lambda b: (0, 0)),
            pl.BlockSpec((1, HEAD_DIM), lambda b: (0, 0)),
            pl.BlockSpec((4, HEAD_DIM), lambda b: (0, 0)),
            pl.BlockSpec((CTX_LEN, IN_WIDTH), lambda b: (b, 0)),
        ],
        out_specs=pl.BlockSpec((CTX_LEN, D_MODEL), lambda b: (b, 0)),
    )(df_lambda, subg, sink, qkv_c)


def _outproj_kernel(*refs, round_next_in_proj):
    if round_next_in_proj:
        (y0_ref, y1_ref, y2_ref, y3_ref, h_ref, mod_ref, g_ref, w32_ref, win_ref,
         o_ref, f_ref, win_out_ref, w_ref) = refs
        win_out_ref[...] = win_ref[...].astype(BF16)
    else:
        y0_ref, y1_ref, y2_ref, y3_ref, h_ref, mod_ref, g_ref, w32_ref, o_ref, f_ref, w_ref = refs

    @pl.when(pl.program_id(0) == 0)
    def _():
        w_ref[...] = w32_ref[...].astype(BF16)

    gate = mod_ref[0, 2:3, :] * g_ref[1:2, :]
    ffn_gain = _gain(g_ref[2:3, :], mod_ref[0, 4:5, :])
    for c in range(TM_PROJ // ROW_SUB):
        rows = slice(c * ROW_SUB, (c + 1) * ROW_SUB)
        y = jnp.concatenate([y_ref[rows, :] for y_ref in (y0_ref, y1_ref, y2_ref, y3_ref)], axis=1)
        h_new = h_ref[rows, :] + _rms(_dot(y, w_ref[...])) * gate
        o_ref[rows, :] = h_new
        f_ref[rows, :] = (_rms(h_new) * ffn_gain + mod_ref[0, 3:4, :]).astype(BF16)


def _out_projection(ys, y_cols, h, mod, mod_row, g, w_out, layer, w_in_next=None):
    rows = h.shape[0]
    tm = TM_PROJ
    n = rows // tm
    y_specs = [pl.BlockSpec((tm, MIX_COLS), lambda i, c=c: (i, c)) for c in y_cols]
    row_block = pl.BlockSpec((tm, D_MODEL), lambda i: (i, 0))
    in_specs = y_specs + [
        row_block,
        pl.BlockSpec((1, 6, D_MODEL), lambda i: (mod_row(i), 0, 0)),
        pl.BlockSpec((4, D_MODEL), lambda i: (0, 0)),
        _resident_layer(layer, (D_MODEL, D_MODEL)),
    ]
    args = [*ys, h, mod, g, w_out]
    out_shape = [jax.ShapeDtypeStruct((rows, D_MODEL), F32), jax.ShapeDtypeStruct((rows, D_MODEL), BF16)]
    out_specs = [row_block, row_block]
    if w_in_next is not None:
        n_slabs = 32
        slab = D_MODEL // n_slabs
        assert n_slabs <= n
        last = n_slabs - 1
        in_specs.append(pl.BlockSpec((None, slab, IN_WIDTH),
                                     lambda i: (layer + 1, jnp.minimum(i, last), 0)))
        args.append(w_in_next)
        out_shape.append(jax.ShapeDtypeStruct((D_MODEL, IN_WIDTH), BF16))
        out_specs.append(pl.BlockSpec((slab, IN_WIDTH), lambda i: (jnp.minimum(i, last), 0)))
    return _call(
        functools.partial(_outproj_kernel, round_next_in_proj=w_in_next is not None),
        "out_projection",
        out_shape=out_shape,
        grid=(n,),
        in_specs=in_specs,
        out_specs=out_specs,
        scratch_shapes=[pltpu.VMEM((D_MODEL, D_MODEL), BF16)],
    )(*args)


def _ffn_up_kernel(*refs, round_down_proj):
    if round_down_proj:
        f_ref, wg_ref, wu_ref, wd_ref, o_ref, wd_out_ref, wg_scr, wu_scr = refs
        wd_out_ref[...] = wd_ref[...].astype(BF16)
    else:
        f_ref, wg_ref, wu_ref, o_ref, wg_scr, wu_scr = refs

    @pl.when(pl.program_id(1) == 0)
    def _():
        wg_scr[...] = wg_ref[...].astype(BF16)
        wu_scr[...] = wu_ref[...].astype(BF16)

    f = f_ref[...]
    gate = _dot(f, wg_scr[...])
    up = _dot(f, wu_scr[...])
    o_ref[...] = (gate * jax.nn.sigmoid(gate) * up).astype(BF16)


def _ffn_up(f, w_gate_up, layer, w_down=None):
    rows = f.shape[0]
    tm, tf = min(TM_UP, rows), TF_UP
    nf, nt = D_FF // tf, rows // tm
    in_specs = [
        pl.BlockSpec((tm, D_MODEL), lambda k, i: (i, 0)),
        pl.BlockSpec((None, D_MODEL, tf), lambda k, i: (layer, 0, k)),
        pl.BlockSpec((None, D_MODEL, tf), lambda k, i: (layer, 0, nf + k)),
    ]
    args = [f, w_gate_up, w_gate_up]
    out_shape = [jax.ShapeDtypeStruct((rows, D_FF), BF16)]
    out_specs = [pl.BlockSpec((tm, tf), lambda k, i: (i, k))]
    if w_down is not None:
        slab = D_FF // (nf * nt)
        assert slab * nf * nt == D_FF and slab % 16 == 0
        in_specs.append(pl.BlockSpec((None, slab, D_MODEL), lambda k, i: (layer, k * nt + i, 0)))
        args.append(w_down)
        out_shape.append(jax.ShapeDtypeStruct((D_FF, D_MODEL), BF16))
        out_specs.append(pl.BlockSpec((slab, D_MODEL), lambda k, i: (k * nt + i, 0)))
    out = _call(
        functools.partial(_ffn_up_kernel, round_down_proj=w_down is not None), "ffn_up",
        out_shape=out_shape,
        grid=(nf, nt),
        in_specs=in_specs,
        out_specs=out_specs,
        scratch_shapes=[pltpu.VMEM((D_MODEL, tf), BF16), pltpu.VMEM((D_MODEL, tf), BF16)],
    )(*args)
    return out if w_down is not None else out[0]


def _ffn_down_kernel(a_ref, h_ref, mod_ref, g_ref, w_ref, o_ref):
    gate = mod_ref[0, 5:6, :] * g_ref[3:4, :]
    for c in range(TM_PROJ // ROW_SUB):
        rows = slice(c * ROW_SUB, (c + 1) * ROW_SUB)
        acc = _dot(a_ref[rows, :], w_ref[...])
        o_ref[rows, :] = h_ref[rows, :] + _rms(acc) * gate


def _ffn_down(act, h, mod, mod_row, g, w_down):
    rows = h.shape[0]
    tm = TM_PROJ
    row_block = pl.BlockSpec((tm, D_MODEL), lambda i: (i, 0))
    return _call(
        _ffn_down_kernel, "ffn_down",
        out_shape=jax.ShapeDtypeStruct((rows, D_MODEL), F32),
        grid=(rows // tm,),
        in_specs=[
            pl.BlockSpec((tm, D_FF), lambda i: (i, 0)),
            row_block,
            pl.BlockSpec((1, 6, D_MODEL), lambda i: (mod_row(i), 0, 0)),
            pl.BlockSpec((4, D_MODEL), lambda i: (0, 0)),
            _resident((D_FF, D_MODEL)),
        ],
        out_specs=row_block,
    )(act, h, mod, g, w_down)


def _rope_table(dim):
    q = dim // 4
    t = jnp.arange(SEQ)
    pos = jnp.stack([t // GRID_W, t % GRID_W], axis=-1).astype(F32)
    inv = ROPE_THETA ** (-jnp.arange(q, dtype=F32) / q)
    ang = pos[:, :, None] * inv
    cos = jnp.broadcast_to(jnp.cos(ang)[:, :, None, :], (SEQ, 2, 2, q)).reshape(SEQ, dim)
    sin = jnp.sin(ang)
    sin = jnp.stack([-sin, sin], axis=2).reshape(SEQ, dim)
    reps = HEAD_DIM // dim
    return jnp.tile(cos, (1, reps)), jnp.tile(sin, (1, reps))


def kernel(x, c, ctx, c_ctx, w_mod, b_mod, norm_g, w_in, w_out, na_rpb, qk_norm_g,
           df_lambda, df_subln_g, sw_sink, w_gate_up, w_down):
    pad_rows = jnp.zeros((MOD_ROWS - 1 - BATCH, D_MODEL), F32)
    cvec = jnp.concatenate([c_ctx[None, :], c, pad_rows], axis=0)
    mod_all = _modulation(cvec, w_mod, b_mod).reshape(DEPTH, MOD_ROWS, 6, D_MODEL)
    rope_tab = jnp.stack(_rope_table(HEAD_DIM) + _rope_table(DF_QK_DIM))
    na_bias = _na_bias_table(na_rpb)

    tiles_per_seq = SEQ // TM_PROJ
    lat_row = lambda i: 1 + i // tiles_per_seq
    ctx_row = lambda i: 0

    w_in_l = w_in[0].astype(BF16)

    h = x.reshape(BATCH * SEQ, D_MODEL)
    hc = ctx.reshape(BATCH * CTX_LEN, D_MODEL)
    for l in range(DEPTH):
        ctx_out = l < DEPTH - 1
        lam_init = 0.8 - 0.6 * math.exp(-0.3 * l)
        mod = mod_all[l]
        g = norm_g[l]
        subg = df_subln_g[l][None, :]
        sink = jnp.broadcast_to(sw_sink[l][:, None], (4, HEAD_DIM))

        qkv = _in_projection(h, mod, lat_row, g, qk_norm_g[l], w_in_l, rope_tab)
        qkv_c = _in_projection(hc, mod, ctx_row, g, qk_norm_g[l], w_in_l, None)

        ys = [
            _na_attention(qkv, qkv_c, na_bias, l),
            _gq_attention(qkv, qkv_c),
            _df_attention(qkv, qkv_c, df_lambda[l], subg, lam_init),
            _sw_attention(qkv, qkv_c, sink),
        ]
        if ctx_out:
            h, f, w_in_l = _out_projection(ys, (0, 0, 0, 0), h, mod, lat_row, g, w_out, l, w_in)
        else:
            h, f = _out_projection(ys, (0, 0, 0, 0), h, mod, lat_row, g, w_out, l)
        act, w_down_l = _ffn_up(f, w_gate_up, l, w_down)
        h = _ffn_down(act, h, mod, lat_row, g, w_down_l)
        if ctx_out:
            yc = _ctx_attention(qkv_c, df_lambda[l], subg, sink, lam_init)
            hc, fc = _out_projection([yc] * 4, (0, 1, 2, 3), hc, mod, ctx_row, g, w_out, l)
            hc = _ffn_down(_ffn_up(fc, w_gate_up, l), hc, mod, ctx_row, g, w_down_l)
    return h.reshape(BATCH, SEQ, D_MODEL)
```

```python
import functools
import math

import jax
import jax.numpy as jnp
import numpy as np
from jax import lax
from jax.experimental import pallas as pl
from jax.experimental.pallas import tpu as pltpu

D_MODEL = 2048
BATCH = 4
SEQ = 4096
DEPTH = 2
GRID_W = 64
GRID_ROWS = SEQ // GRID_W
CTX_LEN = 256
HEAD_DIM = 128
NA_KH = 8
NA_KW = 16
DF_QK_DIM = 64
SW_WINDOW = 128
D_FF = 5632
IN_WIDTH = 5120
ROPE_THETA = 10000.0
EPS = 1e-6
NEG_INF = -1e30
LOG2E = math.log2(math.e)

NA_Q, NA_K, NA_V = 0, 4, 8
GQ_Q, GQ_K, GQ_V = 12, 16, 18
DF_Q, DF_K, DF_V = 20, 24, 28
SW_Q, SW_K, SW_V = 32, 36, 38

F32 = jnp.float32
BF16 = jnp.bfloat16
MIB = 1024 * 1024

TM_PROJ = 512
ROW_SUB = 256
OUT_SUB = 128
TM_UP = 2048
UP_SUB = 1024
TF_UP = 512
TN_MOD = 1536
TQ_FLASH = 512
FLASH_CHUNKS = (2048, 2048)
assert sum(FLASH_CHUNKS) == SEQ
NA_QROWS = 4
NA_KROWS = 12
TQ_NA = NA_QROWS * GRID_W
TK_NA = NA_KROWS * GRID_W
TQ_SW = 256
TK_SW = TQ_SW + 2 * SW_WINDOW
NA_SUB = 16
SW_SUB = 4
MOD_ROWS = 8
MIX_COLS = 4 * HEAD_DIM
ROT_HALF = HEAD_DIM // 4
ROT_HALF_DF = DF_QK_DIM // 4

VMEM_LIMIT_MIB = {
    "modulation": 40, "in_projection": 56, "na_bias": 32, "na_attention": 40,
    "gq_attention": 48, "df_attention": 48, "sw_attention": 40, "ctx_attention": 32,
    "out_projection": 56, "ffn_up": 54, "ffn_down": 58,
}


def _call(kernel, name, **kwargs):
    params = pltpu.CompilerParams(vmem_limit_bytes=VMEM_LIMIT_MIB[name] * MIB)
    return pl.pallas_call(kernel, name=name, compiler_params=params, **kwargs)


def _resident_layer(layer, shape):
    index = (layer,) + (0,) * len(shape)
    return pl.BlockSpec((None,) + tuple(shape), lambda *_: index, pipeline_mode=pl.Buffered(1))


def _resident(shape):
    index = (0,) * len(shape)
    return pl.BlockSpec(tuple(shape), lambda *_: index, pipeline_mode=pl.Buffered(1))


def _gain(g, scale=None):
    return g if scale is None else g * (1.0 + scale)


def _rms(x):
    return x * lax.rsqrt(jnp.mean(x * x, axis=-1, keepdims=True) + EPS)


def _dot(a, b):
    return jnp.dot(a, b, preferred_element_type=F32)


def _dot_nt(a, b):
    return lax.dot_general(a, b, (((1,), (1,)), ((), ())), preferred_element_type=F32)


def _mod_kernel(c_ref, w_ref, b_ref, o_ref):
    c = c_ref[...]
    s = c * jax.nn.sigmoid(c)
    hi = s.astype(BF16)
    lo = (s - hi.astype(F32)).astype(BF16)
    r = _dot(jnp.concatenate([hi, lo], axis=0), w_ref[0].astype(BF16))
    o_ref[0] = r[0:MOD_ROWS] + r[MOD_ROWS:] + b_ref[0]


def _modulation(cvec, w_mod, b_mod):
    n = 6 * D_MODEL
    return _call(
        _mod_kernel, "modulation",
        out_shape=jax.ShapeDtypeStruct((DEPTH, MOD_ROWS, n), F32),
        grid=(DEPTH, n // TN_MOD),
        in_specs=[
            pl.BlockSpec((MOD_ROWS, D_MODEL), lambda l, j: (0, 0)),
            pl.BlockSpec((1, D_MODEL, TN_MOD), lambda l, j: (l, 0, j)),
            pl.BlockSpec((1, 1, TN_MOD), lambda l, j: (l, 0, j)),
        ],
        out_specs=pl.BlockSpec((1, MOD_ROWS, TN_MOD), lambda l, j: (l, 0, j)),
    )(cvec, w_mod, b_mod.reshape(DEPTH, 1, n))


def _swap_halves(x, half):
    lane = lax.broadcasted_iota(jnp.int32, x.shape, 1)
    fwd = pltpu.roll(x, HEAD_DIM - half, axis=1)
    bwd = pltpu.roll(x, half, axis=1)
    return jnp.where((lane & (2 * half - 1)) < half, fwd, bwd)


def _inproj_kernel(*refs, rope):
    if rope:
        h_ref, mod_ref, g_ref, qkg_ref, w_ref, rope_ref, o_ref, a_scr = refs
    else:
        h_ref, mod_ref, g_ref, qkg_ref, w_ref, o_ref, a_scr = refs
        rope_ref = None
    a = _rms(h_ref[...]) * _gain(g_ref[0:1, :], mod_ref[0, 1:2, :]) + mod_ref[0, 0:1, :]
    a_scr[...] = a.astype(BF16)

    def rot(x, table, half):
        if rope_ref is None:
            return x
        return x * rope_ref[table] + _swap_halves(x, half) * rope_ref[table + 1]

    scale = HEAD_DIM ** -0.5 * LOG2E
    scale_df = DF_QK_DIM ** -0.5 * LOG2E
    plain = lambda x: x
    groups = [
        (NA_Q, 4, lambda x: x * scale),
        (NA_K, 4, plain),
        (NA_V, 4, plain),
        (GQ_Q, 4, lambda x: rot(_rms(x) * qkg_ref[0:1, :], 0, ROT_HALF) * scale),
        (GQ_K, 2, lambda x: rot(_rms(x) * qkg_ref[1:2, :], 0, ROT_HALF)),
        (GQ_V, 2, plain),
        (DF_Q, 4, lambda x: rot(x, 2, ROT_HALF_DF) * scale_df),
        (DF_K, 4, lambda x: rot(x, 2, ROT_HALF_DF)),
        (DF_V, 4, plain),
        (SW_Q, 4, lambda x: rot(x, 0, ROT_HALF) * scale),
        (SW_K, 2, lambda x: rot(x, 0, ROT_HALF)),
        (SW_V, 2, plain),
    ]
    for first, count, fn in groups:
        c0 = first * HEAD_DIM
        acc = _dot(a_scr[...], w_ref[:, c0:c0 + count * HEAD_DIM])
        for j in range(count):
            x = fn(acc[:, j * HEAD_DIM:(j + 1) * HEAD_DIM])
            o_ref[:, c0 + j * HEAD_DIM:c0 + (j + 1) * HEAD_DIM] = x.astype(BF16)


def _in_projection(h, mod, mod_row, g, qkg, w_in, rope_tab):
    rows = h.shape[0]
    tm = TM_PROJ
    rope = rope_tab is not None
    in_specs = [
        pl.BlockSpec((tm, D_MODEL), lambda i: (i, 0)),
        pl.BlockSpec((1, 6, D_MODEL), lambda i: (mod_row(i), 0, 0)),
        pl.BlockSpec((4, D_MODEL), lambda i: (0, 0)),
        pl.BlockSpec((2, HEAD_DIM), lambda i: (0, 0)),
        _resident((D_MODEL, IN_WIDTH)),
    ]
    args = [h, mod, g, qkg, w_in]
    if rope:
        tiles_per_seq = SEQ // tm
        in_specs.append(pl.BlockSpec((4, tm, HEAD_DIM), lambda i: (0, i % tiles_per_seq, 0)))
        args.append(rope_tab)
    return _call(
        functools.partial(_inproj_kernel, rope=rope), "in_projection",
        out_shape=jax.ShapeDtypeStruct((rows, IN_WIDTH), BF16),
        grid=(rows // tm,),
        in_specs=in_specs,
        out_specs=pl.BlockSpec((tm, IN_WIDTH), lambda i: (i, 0)),
        scratch_shapes=[pltpu.VMEM((tm, D_MODEL), BF16)],
    )(*args)


def _softmax_pv(s, v, extra_logit=None):
    m = jnp.max(s, axis=-1, keepdims=True)
    if extra_logit is not None:
        m = jnp.maximum(m, extra_logit)
    p = jnp.exp2(s - m)
    l = jnp.sum(p, axis=-1, keepdims=True)
    if extra_logit is not None:
        l = l + jnp.exp2(extra_logit - m)
    return _dot(p.astype(BF16), v) * (1.0 / l)


def _softmax_pv_blocks(scores, values):
    m = functools.reduce(jnp.maximum, [jnp.max(s, axis=-1, keepdims=True) for s in scores])
    l = None
    o = None
    for s, v in zip(scores, values):
        p = jnp.exp2(s - m)
        li = jnp.sum(p, axis=-1, keepdims=True)
        oi = _dot(p.astype(BF16), v)
        l = li if l is None else l + li
        o = oi if o is None else o + oi
    return o * (1.0 / l)


def _rows(*blocks):
    return jnp.concatenate(blocks, axis=0)


def _online_softmax_pv(q, key_chunks, value_chunks):
    n = len(key_chunks)
    s_next = _dot_nt(q, key_chunks[0]())
    m = None
    acc = None
    for i in range(n):
        s = s_next
        if i + 1 < n:
            s_next = _dot_nt(q, key_chunks[i + 1]())
        mi = jnp.max(s, axis=-1, keepdims=True)
        m_new = mi if m is None else jnp.maximum(m, mi)
        pv = _dot(jnp.exp2(s - m_new).astype(BF16), value_chunks[i]())
        acc = pv if acc is None else acc * jnp.exp2(m - m_new) + pv
        m = m_new
    return acc


def _fill_value_ones(vaug, vc_ref, vl_ref):
    vaug[0:CTX_LEN, 0:HEAD_DIM] = vc_ref[...]
    vaug[CTX_LEN:, 0:HEAD_DIM] = vl_ref[...]
    vaug[:, HEAD_DIM:] = jnp.ones((CTX_LEN + SEQ, HEAD_DIM), BF16)


def _flash_chunks(kl_ref, kc_ref, vaug):
    keys, values = [], []
    start = 0
    for size in FLASH_CHUNKS:
        keys.append(lambda a=start, b=start + size: kl_ref[a:b, :])
        values.append(lambda a=start, b=start + size: vaug[CTX_LEN + a:CTX_LEN + b, :])
        start += size
    keys.append(lambda: kc_ref[...])
    values.append(lambda: vaug[0:CTX_LEN, :])
    return keys, values


def _diff_lambda(lam_ref, lam_init):
    lam = lam_ref[...]
    a = jnp.sum(lam[0:1, :] * lam[1:2, :], axis=-1, keepdims=True)
    b = jnp.sum(lam[2:3, :] * lam[3:4, :], axis=-1, keepdims=True)
    return jnp.exp(a) - jnp.exp(b) + lam_init


def _split_diff_query(q):
    lane = lax.broadcasted_iota(jnp.int32, q.shape, 1)
    zero = jnp.zeros_like(q)
    return jnp.concatenate([jnp.where(lane < DF_QK_DIM, q, zero),
                            jnp.where(lane >= DF_QK_DIM, q, zero)], axis=0)


def _na_kernel(q_ref, kl_ref, vl_ref, kc_ref, vc_ref, b0_ref, b1_ref, b2_ref, o_ref):
    step = pl.program_id(2)
    bias_refs = [b0_ref] + [b1_ref] * (NA_SUB - 2) + [b2_ref]

    def scores(j):
        t = NA_SUB * step + j
        first_row = jnp.clip(NA_QROWS * t - NA_KH // 2, 0, GRID_ROWS - NA_KROWS)
        start = pl.multiple_of(first_row * GRID_W, TQ_NA)
        q = q_ref[j * TQ_NA:(j + 1) * TQ_NA, :]
        s_loc = _dot_nt(q, kl_ref[pl.ds(start, TK_NA), :]) + bias_refs[j][0, 0]
        return start, [s_loc, _dot_nt(q, kc_ref[...])]

    nxt = scores(0)
    for j in range(NA_SUB):
        start, s = nxt
        if j + 1 < NA_SUB:
            nxt = scores(j + 1)
        o = _softmax_pv_blocks(s, [vl_ref[pl.ds(start, TK_NA), :], vc_ref[...]])
        o_ref[j * TQ_NA:(j + 1) * TQ_NA, :] = o.astype(BF16)


def _na_window_geometry(t):
    first_row = min(max(NA_QROWS * t - NA_KH // 2, 0), GRID_ROWS - NA_KROWS)
    r = NA_QROWS * t + np.arange(NA_QROWS)
    kr = first_row + np.arange(NA_KROWS)
    c = np.arange(GRID_W)
    kc = np.arange(GRID_W)
    r0 = np.clip(r - NA_KH // 2, 0, GRID_ROWS - NA_KH)
    c0 = np.clip(c - NA_KW // 2, 0, GRID_W - NA_KW)
    ok_r = (kr[None, :] >= r0[:, None]) & (kr[None, :] < r0[:, None] + NA_KH)
    ok_c = (kc[None, :] >= c0[:, None]) & (kc[None, :] < c0[:, None] + NA_KW)
    dr = np.clip(kr[None, :] - r[:, None] + NA_KH - 1, 0, 2 * NA_KH - 2)
    dc = np.clip(kc[None, :] - c[:, None] + NA_KW - 1, 0, 2 * NA_KW - 2)
    return dr, dc, ok_r, ok_c


def _na_bias_kernel(lo_ref, hi_ref, ok_ref, o_ref, *, row_offset):
    ok = ok_ref[...] > 0.0
    for v in range(3):
        for r in range(NA_QROWS):
            for pair in range(NA_KROWS // 2):
                d0 = row_offset[v][r][2 * pair]
                d1 = row_offset[v][r][2 * pair + 1]
                src = lo_ref[d0:d0 + 1, :] + hi_ref[d1:d1 + 1, :]
                block = pltpu.roll(jnp.broadcast_to(src, (GRID_W, 2 * GRID_W)), 0, 1,
                                   stride=1, stride_axis=0)
                o_ref[v, r * GRID_W:(r + 1) * GRID_W, pair * 2 * GRID_W:(pair + 1) * 2 * GRID_W] = (
                    jnp.where(ok, block, NEG_INF))


def _na_bias_table(na_rpb):
    n_dr, n_dc = 2 * NA_KH - 1, 2 * NA_KW - 1
    half = NA_KW - 1
    rpb = jnp.pad(na_rpb.astype(F32) * LOG2E, ((0, 0), (0, 0), (0, 1), (0, 0)),
                  constant_values=NEG_INF)
    lanes = 2 * GRID_W
    lo = jnp.roll(jnp.pad(rpb, ((0, 0),) * 3 + ((0, lanes - n_dc),)), -half, axis=-1)
    hi = jnp.pad(rpb, ((0, 0),) * 3 + ((GRID_W - half, lanes - n_dc - GRID_W + half),))
    geometry = [_na_window_geometry(t) for t in (0, 1, GRID_ROWS // NA_QROWS - 1)]
    ok_c = geometry[0][3]
    row_offset = [np.where(g[2], g[0], n_dr).tolist() for g in geometry]
    ok = jnp.asarray(np.concatenate([ok_c, ok_c], axis=1).astype(np.float32))
    src_spec = pl.BlockSpec((None, None, n_dr + 1, lanes), lambda l, h: (l, h, 0, 0))
    return _call(
        functools.partial(_na_bias_kernel, row_offset=row_offset), "na_bias",
        out_shape=jax.ShapeDtypeStruct((DEPTH, 3, 4, TQ_NA, TK_NA), F32),
        grid=(DEPTH, 4),
        in_specs=[src_spec, src_spec, pl.BlockSpec((GRID_W, lanes), lambda l, h: (0, 0))],
        out_specs=pl.BlockSpec((None, 3, None, TQ_NA, TK_NA), lambda l, h: (l, 0, h, 0, 0)),
    )(lo, hi, ok)


def _na_attention(qkv, qkv_c, bias, layer):
    tq = NA_SUB * TQ_NA
    nt = SEQ // tq
    kv = lambda col: pl.BlockSpec((SEQ, HEAD_DIM), lambda b, h, t: (b, col + h))
    kvc = lambda col: pl.BlockSpec((CTX_LEN, HEAD_DIM), lambda b, h, t: (b, col + h))
    bias_block = (None, 1, 1, TQ_NA, TK_NA)
    return _call(
        _na_kernel, "na_attention",
        out_shape=jax.ShapeDtypeStruct((BATCH * SEQ, 4 * HEAD_DIM), BF16),
        grid=(BATCH, 4, nt),
        in_specs=[
            pl.BlockSpec((tq, HEAD_DIM), lambda b, h, t: (b * nt + t, NA_Q + h)),
            kv(NA_K), kv(NA_V), kvc(NA_K), kvc(NA_V),
            pl.BlockSpec(bias_block, lambda b, h, t: (layer, jnp.where(t == 0, 0, 1), h, 0, 0)),
            pl.BlockSpec(bias_block, lambda b, h, t: (layer, 1, h, 0, 0)),
            pl.BlockSpec(bias_block, lambda b, h, t: (layer, jnp.where(t == nt - 1, 2, 1), h, 0, 0)),
        ],
        out_specs=pl.BlockSpec((tq, HEAD_DIM), lambda b, h, t: (b * nt + t, h)),
    )(qkv, qkv, qkv, qkv_c, qkv_c, bias, bias, bias)


def _gq_kernel(q_ref, kl_ref, vl_ref, kc_ref, vc_ref, o_ref, vaug):
    @pl.when(pl.program_id(2) == 0)
    def _():
        _fill_value_ones(vaug, vc_ref, vl_ref)

    q = jnp.concatenate([q_ref[:, 0:HEAD_DIM], q_ref[:, HEAD_DIM:]], axis=0)
    acc = _online_softmax_pv(q, *_flash_chunks(kl_ref, kc_ref, vaug))
    o = acc[:, 0:HEAD_DIM] / acc[:, HEAD_DIM:]
    o_ref[:, 0:HEAD_DIM] = o[0:TQ_FLASH].astype(BF16)
    o_ref[:, HEAD_DIM:] = o[TQ_FLASH:].astype(BF16)


def _gq_attention(qkv, qkv_c):
    nt = SEQ // TQ_FLASH
    kv = lambda col: pl.BlockSpec((SEQ, HEAD_DIM), lambda b, k, t: (b, col + k))
    kvc = lambda col: pl.BlockSpec((CTX_LEN, HEAD_DIM), lambda b, k, t: (b, col + k))
    return _call(
        _gq_kernel, "gq_attention",
        out_shape=jax.ShapeDtypeStruct((BATCH * SEQ, 4 * HEAD_DIM), BF16),
        grid=(BATCH, 2, nt),
        in_specs=[
            pl.BlockSpec((TQ_FLASH, 2 * HEAD_DIM), lambda b, k, t: (b * nt + t, GQ_Q // 2 + k)),
            kv(GQ_K), kv(GQ_V), kvc(GQ_K), kvc(GQ_V),
        ],
        out_specs=pl.BlockSpec((TQ_FLASH, 2 * HEAD_DIM), lambda b, k, t: (b * nt + t, k)),
        scratch_shapes=[pltpu.VMEM((CTX_LEN + SEQ, 2 * HEAD_DIM), BF16)],
    )(qkv, qkv, qkv, qkv_c, qkv_c)


def _df_kernel(lam_ref, subg_ref, q_ref, kl_ref, vl_ref, kc_ref, vc_ref, o_ref, vaug, *, lam_init):
    @pl.when(pl.program_id(2) == 0)
    def _():
        _fill_value_ones(vaug, vc_ref, vl_ref)

    lam = _diff_lambda(lam_ref, lam_init)
    acc = _online_softmax_pv(_split_diff_query(q_ref[...]), *_flash_chunks(kl_ref, kc_ref, vaug))
    o = acc[:, 0:HEAD_DIM] / acc[:, HEAD_DIM:]
    o = o[0:TQ_FLASH] - lam * o[TQ_FLASH:]
    o_ref[...] = (_rms(o) * (subg_ref[...] * (1.0 - lam_init))).astype(BF16)


def _df_attention(qkv, qkv_c, df_lambda, subg, lam_init):
    nt = SEQ // TQ_FLASH
    kv = lambda col: pl.BlockSpec((SEQ, HEAD_DIM), lambda b, h, t: (b, col + h))
    kvc = lambda col: pl.BlockSpec((CTX_LEN, HEAD_DIM), lambda b, h, t: (b, col + h))
    return _call(
        functools.partial(_df_kernel, lam_init=lam_init), "df_attention",
        out_shape=jax.ShapeDtypeStruct((BATCH * SEQ, 4 * HEAD_DIM), BF16),
        grid=(BATCH, 4, nt),
        in_specs=[
            pl.BlockSpec((4, DF_QK_DIM), lambda b, h, t: (0, 0)),
            pl.BlockSpec((1, HEAD_DIM), lambda b, h, t: (0, 0)),
            pl.BlockSpec((TQ_FLASH, HEAD_DIM), lambda b, h, t: (b * nt + t, DF_Q + h)),
            kv(DF_K), kv(DF_V), kvc(DF_K), kvc(DF_V),
        ],
        out_specs=pl.BlockSpec((TQ_FLASH, HEAD_DIM), lambda b, h, t: (b * nt + t, h)),
        scratch_shapes=[pltpu.VMEM((CTX_LEN + SEQ, 2 * HEAD_DIM), BF16)],
    )(df_lambda, subg, qkv, qkv, qkv, qkv_c, qkv_c)


def _sw_kernel(sink_ref, q_ref, kl_ref, vl_ref, kc_ref, vc_ref, m0_ref, m1_ref, m2_ref, o_ref):
    step = pl.program_id(2)
    mask_refs = [m0_ref] + [m1_ref] * (SW_SUB - 2) + [m2_ref]
    head = lax.broadcasted_iota(jnp.int32, (2 * TQ_SW, 1), 0) >= TQ_SW
    sink = jnp.where(head, sink_ref[0, 1:2, 0:1], sink_ref[0, 0:1, 0:1]) * LOG2E

    for j in range(SW_SUB):
        t = SW_SUB * step + j
        start = pl.multiple_of(jnp.clip(TQ_SW * t - SW_WINDOW, 0, SEQ - TK_SW), SW_WINDOW)
        keys = _rows(kl_ref[pl.ds(start, TK_SW), :], kc_ref[...])
        values = _rows(vl_ref[pl.ds(start, TK_SW), :], vc_ref[...])
        rows = slice(j * TQ_SW, (j + 1) * TQ_SW)
        q = _rows(q_ref[rows, 0:HEAD_DIM], q_ref[rows, HEAD_DIM:])
        o = _softmax_pv(_dot_nt(q, keys) + mask_refs[j][0], values, extra_logit=sink)
        o_ref[rows, 0:HEAD_DIM] = o[0:TQ_SW].astype(BF16)
        o_ref[rows, HEAD_DIM:] = o[TQ_SW:].astype(BF16)


def _sw_mask_table():
    r = np.arange(TQ_SW)[:, None]
    c = np.arange(TK_SW)[None, :]
    nt = SEQ // TQ_SW
    tables = []
    for t in (0, 1, nt - 1):
        start = min(max(TQ_SW * t - SW_WINDOW, 0), SEQ - TK_SW)
        ok = np.abs((TQ_SW * t + r) - (start + c)) <= SW_WINDOW
        band = np.where(ok, 0.0, NEG_INF).astype(np.float32)
        band = np.concatenate([band, np.zeros((TQ_SW, CTX_LEN), np.float32)], axis=1)
        tables.append(np.concatenate([band, band], axis=0))
    return jnp.asarray(np.stack(tables))


def _sw_attention(qkv, qkv_c, sink):
    tq = SW_SUB * TQ_SW
    nt = SEQ // tq
    mask = _sw_mask_table()
    kv = lambda col: pl.BlockSpec((SEQ, HEAD_DIM), lambda b, k, t: (b, col + k))
    kvc = lambda col: pl.BlockSpec((CTX_LEN, HEAD_DIM), lambda b, k, t: (b, col + k))
    mask_block = (1, 2 * TQ_SW, TK_SW + CTX_LEN)
    return _call(
        _sw_kernel, "sw_attention",
        out_shape=jax.ShapeDtypeStruct((BATCH * SEQ, 4 * HEAD_DIM), BF16),
        grid=(BATCH, 2, nt),
        in_specs=[
            pl.BlockSpec((1, 2, HEAD_DIM), lambda b, k, t: (k, 0, 0)),
            pl.BlockSpec((tq, 2 * HEAD_DIM), lambda b, k, t: (b * nt + t, SW_Q // 2 + k)),
            kv(SW_K), kv(SW_V), kvc(SW_K), kvc(SW_V),
            pl.BlockSpec(mask_block, lambda b, k, t: (jnp.where(t == 0, 0, 1), 0, 0)),
            pl.BlockSpec(mask_block, lambda b, k, t: (1, 0, 0)),
            pl.BlockSpec(mask_block, lambda b, k, t: (jnp.where(t == nt - 1, 2, 1), 0, 0)),
        ],
        out_specs=pl.BlockSpec((tq, 2 * HEAD_DIM), lambda b, k, t: (b * nt + t, k)),
    )(sink.reshape(2, 2, HEAD_DIM), qkv, qkv, qkv, qkv_c, qkv_c, mask, mask, mask)


def _ctx_attn_kernel(lam_ref, subg_ref, sink_ref, x_ref, o_ref, *, lam_init):
    chunk = lambda j: x_ref[:, j * HEAD_DIM:(j + 1) * HEAD_DIM]

    def put(j, o):
        o_ref[:, j * HEAD_DIM:(j + 1) * HEAD_DIM] = o.astype(BF16)

    lam = _diff_lambda(lam_ref, lam_init)
    for h in range(4):
        put(h, _softmax_pv(_dot_nt(chunk(NA_Q + h), chunk(NA_K + h)), chunk(NA_V + h)))
        put(4 + h, _softmax_pv(_dot_nt(chunk(GQ_Q + h), chunk(GQ_K + h // 2)),
                               chunk(GQ_V + h // 2)))
        q2 = _split_diff_query(chunk(DF_Q + h))
        o = _softmax_pv(_dot_nt(q2, chunk(DF_K + h)), chunk(DF_V + h))
        o = o[0:CTX_LEN] - lam * o[CTX_LEN:]
        put(8 + h, _rms(o) * (subg_ref[...] * (1.0 - lam_init)))
        put(12 + h, _softmax_pv(_dot_nt(chunk(SW_Q + h), chunk(SW_K + h // 2)),
                                chunk(SW_V + h // 2),
                                extra_logit=sink_ref[h:h + 1, 0:1] * LOG2E))


def _ctx_attention(qkv_c, df_lambda, subg, sink, lam_init):
    return _call(
        functools.partial(_ctx_attn_kernel, lam_init=lam_init), "ctx_attention",
        out_shape=jax.ShapeDtypeStruct((BATCH * CTX_LEN, D_MODEL), BF16),
        grid=(BATCH,),
        in_specs=[
            pl.BlockSpec((4, DF_QK_DIM), lambda b: (0, 0)),
            pl.BlockSpec((1, HEAD_DIM), lambda b: (0, 0)),
            pl.BlockSpec((4, HEAD_DIM), lambda b: (0, 0)),
            pl.BlockSpec((CTX_LEN, IN_WIDTH), lambda b: (b, 0)),
        ],
        out_specs=pl.BlockSpec((CTX_LEN, D_MODEL), lambda b: (b, 0)),
    )(df_lambda, subg, sink, qkv_c)


def _outproj_kernel(*refs, round_next_in_proj):
    if round_next_in_proj:
        (y0_ref, y1_ref, y2_ref, y3_ref, h_ref, mod_ref, g_ref, w32_ref, win_ref,
         o_ref, f_ref, win_out_ref, w_ref) = refs
        win_out_ref[...] = win_ref[...].astype(BF16)
    else:
        y0_ref, y1_ref, y2_ref, y3_ref, h_ref, mod_ref, g_ref, w32_ref, o_ref, f_ref, w_ref = refs

    @pl.when(pl.program_id(0) == 0)
    def _():
        w_ref[...] = w32_ref[...].astype(BF16)

    gate = mod_ref[0, 2:3, :] * g_ref[1:2, :]
    ffn_gain = _gain(g_ref[2:3, :], mod_ref[0, 4:5, :])
    for c in range(TM_PROJ // OUT_SUB):
        rows = slice(c * OUT_SUB, (c + 1) * OUT_SUB)
        y = jnp.concatenate([y_ref[rows, :] for y_ref in (y0_ref, y1_ref, y2_ref, y3_ref)], axis=1)
        h_new = h_ref[rows, :] + _rms(_dot(y, w_ref[...])) * gate
        o_ref[rows, :] = h_new
        f_ref[rows, :] = (_rms(h_new) * ffn_gain + mod_ref[0, 3:4, :]).astype(BF16)


def _out_projection(ys, y_cols, h, mod, mod_row, g, w_out, layer, w_in_next=None):
    rows = h.shape[0]
    tm = TM_PROJ
    n = rows // tm
    y_specs = [pl.BlockSpec((tm, MIX_COLS), lambda i, c=c: (i, c)) for c in y_cols]
    row_block = pl.BlockSpec((tm, D_MODEL), lambda i: (i, 0))
    in_specs = y_specs + [
        row_block,
        pl.BlockSpec((1, 6, D_MODEL), lambda i: (mod_row(i), 0, 0)),
        pl.BlockSpec((4, D_MODEL), lambda i: (0, 0)),
        _resident_layer(layer, (D_MODEL, D_MODEL)),
    ]
    args = [*ys, h, mod, g, w_out]
    out_shape = [jax.ShapeDtypeStruct((rows, D_MODEL), F32), jax.ShapeDtypeStruct((rows, D_MODEL), BF16)]
    out_specs = [row_block, row_block]
    if w_in_next is not None:
        n_slabs = 32
        slab = D_MODEL // n_slabs
        assert n_slabs <= n
        last = n_slabs - 1
        in_specs.append(pl.BlockSpec((None, slab, IN_WIDTH),
                                     lambda i: (layer + 1, jnp.minimum(i, last), 0)))
        args.append(w_in_next)
        out_shape.append(jax.ShapeDtypeStruct((D_MODEL, IN_WIDTH), BF16))
        out_specs.append(pl.BlockSpec((slab, IN_WIDTH), lambda i: (jnp.minimum(i, last), 0)))
    return _call(
        functools.partial(_outproj_kernel, round_next_in_proj=w_in_next is not None),
        "out_projection",
        out_shape=out_shape,
        grid=(n,),
        in_specs=in_specs,
        out_specs=out_specs,
        scratch_shapes=[pltpu.VMEM((D_MODEL, D_MODEL), BF16)],
    )(*args)


def _ffn_up_kernel(*refs, round_down_proj):
    if round_down_proj:
        f_ref, wg_ref, wu_ref, wd_ref, o_ref, wd_out_ref, wg_scr, wu_scr = refs
        wd_out_ref[...] = wd_ref[...].astype(BF16)
    else:
        f_ref, wg_ref, wu_ref, o_ref, wg_scr, wu_scr = refs

    @pl.when(pl.program_id(1) == 0)
    def _():
        wg_scr[...] = wg_ref[...].astype(BF16)
        wu_scr[...] = wu_ref[...].astype(BF16)

    for c in range(f_ref.shape[0] // UP_SUB):
        rows = slice(c * UP_SUB, (c + 1) * UP_SUB)
        f = f_ref[rows, :]
        gate = _dot(f, wg_scr[...])
        up = _dot(f, wu_scr[...])
        o_ref[rows, :] = (gate * jax.nn.sigmoid(gate) * up).astype(BF16)


def _ffn_up(f, w_gate_up, layer, w_down=None):
    rows = f.shape[0]
    tm, tf = min(TM_UP, rows), TF_UP
    nf, nt = D_FF // tf, rows // tm
    in_specs = [
        pl.BlockSpec((tm, D_MODEL), lambda k, i: (i, 0)),
        pl.BlockSpec((None, D_MODEL, tf), lambda k, i: (layer, 0, k)),
        pl.BlockSpec((None, D_MODEL, tf), lambda k, i: (layer, 0, nf + k)),
    ]
    args = [f, w_gate_up, w_gate_up]
    out_shape = [jax.ShapeDtypeStruct((rows, D_FF), BF16)]
    out_specs = [pl.BlockSpec((tm, tf), lambda k, i: (i, k))]
    if w_down is not None:
        slab = D_FF // (nf * nt)
        assert slab * nf * nt == D_FF and slab % 16 == 0
        in_specs.append(pl.BlockSpec((None, slab, D_MODEL), lambda k, i: (layer, k * nt + i, 0)))
        args.append(w_down)
        out_shape.append(jax.ShapeDtypeStruct((D_FF, D_MODEL), BF16))
        out_specs.append(pl.BlockSpec((slab, D_MODEL), lambda k, i: (k * nt + i, 0)))
    out = _call(
        functools.partial(_ffn_up_kernel, round_down_proj=w_down is not None), "ffn_up",
        out_shape=out_shape,
        grid=(nf, nt),
        in_specs=in_specs,
        out_specs=out_specs,
        scratch_shapes=[pltpu.VMEM((D_MODEL, tf), BF16), pltpu.VMEM((D_MODEL, tf), BF16)],
    )(*args)
    return out if w_down is not None else out[0]


def _ffn_down_kernel(a_ref, h_ref, mod_ref, g_ref, w_ref, o_ref):
    gate = mod_ref[0, 5:6, :] * g_ref[3:4, :]
    for c in range(TM_PROJ // ROW_SUB):
        rows = slice(c * ROW_SUB, (c + 1) * ROW_SUB)
        acc = _dot(a_ref[rows, :], w_ref[...])
        o_ref[rows, :] = h_ref[rows, :] + _rms(acc) * gate


def _ffn_down(act, h, mod, mod_row, g, w_down):
    rows = h.shape[0]
    tm = TM_PROJ
    row_block = pl.BlockSpec((tm, D_MODEL), lambda i: (i, 0))
    return _call(
        _ffn_down_kernel, "ffn_down",
        out_shape=jax.ShapeDtypeStruct((rows, D_MODEL), F32),
        grid=(rows // tm,),
        in_specs=[
            pl.BlockSpec((tm, D_FF), lambda i: (i, 0)),
            row_block,
            pl.BlockSpec((1, 6, D_MODEL), lambda i: (mod_row(i), 0, 0)),
            pl.BlockSpec((4, D_MODEL), lambda i: (0, 0)),
            _resident((D_FF, D_MODEL)),
        ],
        out_specs=row_block,
    )(act, h, mod, g, w_down)


def _rope_table(dim):
    q = dim // 4
    t = jnp.arange(SEQ)
    pos = jnp.stack([t // GRID_W, t % GRID_W], axis=-1).astype(F32)
    inv = ROPE_THETA ** (-jnp.arange(q, dtype=F32) / q)
    ang = pos[:, :, None] * inv
    cos = jnp.broadcast_to(jnp.cos(ang)[:, :, None, :], (SEQ, 2, 2, q)).reshape(SEQ, dim)
    sin = jnp.sin(ang)
    sin = jnp.stack([-sin, sin], axis=2).reshape(SEQ, dim)
    reps = HEAD_DIM // dim
    return jnp.tile(cos, (1, reps)), jnp.tile(sin, (1, reps))


def kernel(x, c, ctx, c_ctx, w_mod, b_mod, norm_g, w_in, w_out, na_rpb, qk_norm_g,
           df_lambda, df_subln_g, sw_sink, w_gate_up, w_down):
    pad_rows = jnp.zeros((MOD_ROWS - 1 - BATCH, D_MODEL), F32)
    cvec = jnp.concatenate([c_ctx[None, :], c, pad_rows], axis=0)
    mod_all = _modulation(cvec, w_mod, b_mod).reshape(DEPTH, MOD_ROWS, 6, D_MODEL)
    rope_tab = jnp.stack(_rope_table(HEAD_DIM) + _rope_table(DF_QK_DIM))
    na_bias = _na_bias_table(na_rpb)

    tiles_per_seq = SEQ // TM_PROJ
    lat_row = lambda i: 1 + i // tiles_per_seq
    ctx_row = lambda i: 0

    w_in_l = w_in[0].astype(BF16)

    h = x.reshape(BATCH * SEQ, D_MODEL)
    hc = ctx.reshape(BATCH * CTX_LEN, D_MODEL)
    for l in range(DEPTH):
        ctx_out = l < DEPTH - 1
        lam_init = 0.8 - 0.6 * math.exp(-0.3 * l)
        mod = mod_all[l]
        g = norm_g[l]
        subg = df_subln_g[l][None, :]
        sink = jnp.broadcast_to(sw_sink[l][:, None], (4, HEAD_DIM))

        qkv = _in_projection(h, mod, lat_row, g, qk_norm_g[l], w_in_l, rope_tab)
        qkv_c = _in_projection(hc, mod, ctx_row, g, qk_norm_g[l], w_in_l, None)

        ys = [
            _na_attention(qkv, qkv_c, na_bias, l),
            _gq_attention(qkv, qkv_c),
            _df_attention(qkv, qkv_c, df_lambda[l], subg, lam_init),
            _sw_attention(qkv, qkv_c, sink),
        ]
        if ctx_out:
            h, f, w_in_l = _out_projection(ys, (0, 0, 0, 0), h, mod, lat_row, g, w_out, l, w_in)
        else:
            h, f = _out_projection(ys, (0, 0, 0, 0), h, mod, lat_row, g, w_out, l)
        act, w_down_l = _ffn_up(f, w_gate_up, l, w_down)
        h = _ffn_down(act, h, mod, lat_row, g, w_down_l)
        if ctx_out:
            yc = _ctx_attention(qkv_c, df_lambda[l], subg, sink, lam_init)
            hc, fc = _out_projection([yc] * 4, (0, 1, 2, 3), hc, mod, ctx_row, g, w_out, l)
            hc = _ffn_down(_ffn_up(fc, w_gate_up, l), hc, mod, ctx_row, g, w_down_l)
    return h.reshape(BATCH, SEQ, D_MODEL)
```

```python
import functools
import math

import jax
import jax.numpy as jnp
import numpy as np
from jax import lax
from jax.experimental import pallas as pl
from jax.experimental.pallas import tpu as pltpu

D_MODEL = 2048
BATCH = 4
SEQ = 4096
DEPTH = 2
GRID_W = 64
GRID_ROWS = SEQ // GRID_W
CTX_LEN = 256
HEAD_DIM = 128
NA_KH = 8
NA_KW = 16
DF_QK_DIM = 64
SW_WINDOW = 128
D_FF = 5632
IN_WIDTH = 5120
ROPE_THETA = 10000.0
EPS = 1e-6
NEG_INF = -1e30
LOG2E = math.log2(math.e)

NA_Q, NA_K, NA_V = 0, 4, 8
GQ_Q, GQ_K, GQ_V = 12, 16, 18
DF_Q, DF_K, DF_V = 20, 24, 28
SW_Q, SW_K, SW_V = 32, 36, 38

F32 = jnp.float32
BF16 = jnp.bfloat16
MIB = 1024 * 1024

TM_PROJ = 512
ROW_SUB = 256
OUT_SUB = 128
TM_UP = 2048
UP_SUB = 1024
TF_UP = 512
TN_MOD = 1536
TQ_FLASH = 512
FLASH_SUB = 2
FLASH_CHUNKS = (2048, 2048)
assert sum(FLASH_CHUNKS) == SEQ
NA_QROWS = 4
NA_KROWS = 12
TQ_NA = NA_QROWS * GRID_W
TK_NA = NA_KROWS * GRID_W
TQ_SW = 256
TK_SW = TQ_SW + 2 * SW_WINDOW
NA_SUB = 16
SW_SUB = 4
MOD_ROWS = 8
MIX_COLS = 4 * HEAD_DIM
ROT_HALF = HEAD_DIM // 4
ROT_HALF_DF = DF_QK_DIM // 4

VMEM_LIMIT_MIB = {
    "modulation": 40, "in_projection": 56, "na_bias": 32, "na_attention": 40,
    "gq_attention": 48, "df_attention": 48, "sw_attention": 40, "ctx_attention": 32,
    "out_projection": 56, "ffn_up": 54, "ffn_down": 58,
}


def _call(kernel, name, **kwargs):
    params = pltpu.CompilerParams(vmem_limit_bytes=VMEM_LIMIT_MIB[name] * MIB)
    return pl.pallas_call(kernel, name=name, compiler_params=params, **kwargs)


def _resident_layer(layer, shape):
    index = (layer,) + (0,) * len(shape)
    return pl.BlockSpec((None,) + tuple(shape), lambda *_: index, pipeline_mode=pl.Buffered(1))


def _resident(shape):
    index = (0,) * len(shape)
    return pl.BlockSpec(tuple(shape), lambda *_: index, pipeline_mode=pl.Buffered(1))


def _gain(g, scale=None):
    return g if scale is None else g * (1.0 + scale)


def _rms(x):
    return x * lax.rsqrt(jnp.mean(x * x, axis=-1, keepdims=True) + EPS)


def _dot(a, b):
    return jnp.dot(a, b, preferred_element_type=F32)


def _dot_nt(a, b):
    return lax.dot_general(a, b, (((1,), (1,)), ((), ())), preferred_element_type=F32)


def _mod_kernel(c_ref, w_ref, b_ref, o_ref):
    c = c_ref[...]
    s = c * jax.nn.sigmoid(c)
    hi = s.astype(BF16)
    lo = (s - hi.astype(F32)).astype(BF16)
    r = _dot(jnp.concatenate([hi, lo], axis=0), w_ref[0].astype(BF16))
    o_ref[0] = r[0:MOD_ROWS] + r[MOD_ROWS:] + b_ref[0]


def _modulation(cvec, w_mod, b_mod):
    n = 6 * D_MODEL
    return _call(
        _mod_kernel, "modulation",
        out_shape=jax.ShapeDtypeStruct((DEPTH, MOD_ROWS, n), F32),
        grid=(DEPTH, n // TN_MOD),
        in_specs=[
            pl.BlockSpec((MOD_ROWS, D_MODEL), lambda l, j: (0, 0)),
            pl.BlockSpec((1, D_MODEL, TN_MOD), lambda l, j: (l, 0, j)),
            pl.BlockSpec((1, 1, TN_MOD), lambda l, j: (l, 0, j)),
        ],
        out_specs=pl.BlockSpec((1, MOD_ROWS, TN_MOD), lambda l, j: (l, 0, j)),
    )(cvec, w_mod, b_mod.reshape(DEPTH, 1, n))


def _swap_halves(x, half):
    lane = lax.broadcasted_iota(jnp.int32, x.shape, 1)
    fwd = pltpu.roll(x, HEAD_DIM - half, axis=1)
    bwd = pltpu.roll(x, half, axis=1)
    return jnp.where((lane & (2 * half - 1)) < half, fwd, bwd)


def _inproj_kernel(*refs, rope):
    if rope:
        h_ref, mod_ref, g_ref, qkg_ref, w_ref, rope_ref, o_ref, a_scr = refs
    else:
        h_ref, mod_ref, g_ref, qkg_ref, w_ref, o_ref, a_scr = refs
        rope_ref = None
    a = _rms(h_ref[...]) * _gain(g_ref[0:1, :], mod_ref[0, 1:2, :]) + mod_ref[0, 0:1, :]
    a_scr[...] = a.astype(BF16)

    def rot(x, table, half):
        if rope_ref is None:
            return x
        return x * rope_ref[table] + _swap_halves(x, half) * rope_ref[table + 1]

    scale = HEAD_DIM ** -0.5 * LOG2E
    scale_df = DF_QK_DIM ** -0.5 * LOG2E
    plain = lambda x: x
    groups = [
        (NA_Q, 4, lambda x: x * scale),
        (NA_K, 4, plain),
        (NA_V, 4, plain),
        (GQ_Q, 4, lambda x: rot(_rms(x) * qkg_ref[0:1, :], 0, ROT_HALF) * scale),
        (GQ_K, 2, lambda x: rot(_rms(x) * qkg_ref[1:2, :], 0, ROT_HALF)),
        (GQ_V, 2, plain),
        (DF_Q, 4, lambda x: rot(x, 2, ROT_HALF_DF) * scale_df),
        (DF_K, 4, lambda x: rot(x, 2, ROT_HALF_DF)),
        (DF_V, 4, plain),
        (SW_Q, 4, lambda x: rot(x, 0, ROT_HALF) * scale),
        (SW_K, 2, lambda x: rot(x, 0, ROT_HALF)),
        (SW_V, 2, plain),
    ]
    for first, count, fn in groups:
        c0 = first * HEAD_DIM
        acc = _dot(a_scr[...], w_ref[:, c0:c0 + count * HEAD_DIM])
        for j in range(count):
            x = fn(acc[:, j * HEAD_DIM:(j + 1) * HEAD_DIM])
            o_ref[:, c0 + j * HEAD_DIM:c0 + (j + 1) * HEAD_DIM] = x.astype(BF16)


def _in_projection(h, mod, mod_row, g, qkg, w_in, rope_tab):
    rows = h.shape[0]
    tm = TM_PROJ
    rope = rope_tab is not None
    in_specs = [
        pl.BlockSpec((tm, D_MODEL), lambda i: (i, 0)),
        pl.BlockSpec((1, 6, D_MODEL), lambda i: (mod_row(i), 0, 0)),
        pl.BlockSpec((4, D_MODEL), lambda i: (0, 0)),
        pl.BlockSpec((2, HEAD_DIM), lambda i: (0, 0)),
        _resident((D_MODEL, IN_WIDTH)),
    ]
    args = [h, mod, g, qkg, w_in]
    if rope:
        tiles_per_seq = SEQ // tm
        in_specs.append(pl.BlockSpec((4, tm, HEAD_DIM), lambda i: (0, i % tiles_per_seq, 0)))
        args.append(rope_tab)
    return _call(
        functools.partial(_inproj_kernel, rope=rope), "in_projection",
        out_shape=jax.ShapeDtypeStruct((rows, IN_WIDTH), BF16),
        grid=(rows // tm,),
        in_specs=in_specs,
        out_specs=pl.BlockSpec((tm, IN_WIDTH), lambda i: (i, 0)),
        scratch_shapes=[pltpu.VMEM((tm, D_MODEL), BF16)],
    )(*args)


def _softmax_pv(s, v, extra_logit=None):
    m = jnp.max(s, axis=-1, keepdims=True)
    if extra_logit is not None:
        m = jnp.maximum(m, extra_logit)
    p = jnp.exp2(s - m)
    l = jnp.sum(p, axis=-1, keepdims=True)
    if extra_logit is not None:
        l = l + jnp.exp2(extra_logit - m)
    return _dot(p.astype(BF16), v) * (1.0 / l)


def _softmax_pv_blocks(scores, values):
    m = functools.reduce(jnp.maximum, [jnp.max(s, axis=-1, keepdims=True) for s in scores])
    l = None
    o = None
    for s, v in zip(scores, values):
        p = jnp.exp2(s - m)
        li = jnp.sum(p, axis=-1, keepdims=True)
        oi = _dot(p.astype(BF16), v)
        l = li if l is None else l + li
        o = oi if o is None else o + oi
    return o * (1.0 / l)


def _rows(*blocks):
    return jnp.concatenate(blocks, axis=0)


def _online_softmax_pv(q, key_chunks, value_chunks):
    n = len(key_chunks)
    s_next = _dot_nt(q, key_chunks[0]())
    m = None
    acc = None
    for i in range(n):
        s = s_next
        if i + 1 < n:
            s_next = _dot_nt(q, key_chunks[i + 1]())
        mi = jnp.max(s, axis=-1, keepdims=True)
        m_new = mi if m is None else jnp.maximum(m, mi)
        pv = _dot(jnp.exp2(s - m_new).astype(BF16), value_chunks[i]())
        acc = pv if acc is None else acc * jnp.exp2(m - m_new) + pv
        m = m_new
    return acc


def _fill_value_ones(vaug, vc_ref, vl_ref):
    vaug[0:CTX_LEN, 0:HEAD_DIM] = vc_ref[...]
    vaug[CTX_LEN:, 0:HEAD_DIM] = vl_ref[...]
    vaug[:, HEAD_DIM:] = jnp.ones((CTX_LEN + SEQ, HEAD_DIM), BF16)


def _flash_chunks(kl_ref, kc_ref, vaug):
    keys, values = [], []
    start = 0
    for size in FLASH_CHUNKS:
        keys.append(lambda a=start, b=start + size: kl_ref[a:b, :])
        values.append(lambda a=start, b=start + size: vaug[CTX_LEN + a:CTX_LEN + b, :])
        start += size
    keys.append(lambda: kc_ref[...])
    values.append(lambda: vaug[0:CTX_LEN, :])
    return keys, values


def _diff_lambda(lam_ref, lam_init):
    lam = lam_ref[...]
    a = jnp.sum(lam[0:1, :] * lam[1:2, :], axis=-1, keepdims=True)
    b = jnp.sum(lam[2:3, :] * lam[3:4, :], axis=-1, keepdims=True)
    return jnp.exp(a) - jnp.exp(b) + lam_init


def _split_diff_query(q):
    lane = lax.broadcasted_iota(jnp.int32, q.shape, 1)
    zero = jnp.zeros_like(q)
    return jnp.concatenate([jnp.where(lane < DF_QK_DIM, q, zero),
                            jnp.where(lane >= DF_QK_DIM, q, zero)], axis=0)


def _na_kernel(q_ref, kl_ref, vl_ref, kc_ref, vc_ref, b0_ref, b1_ref, b2_ref, o_ref):
    step = pl.program_id(2)
    bias_refs = [b0_ref] + [b1_ref] * (NA_SUB - 2) + [b2_ref]

    def scores(j):
        t = NA_SUB * step + j
        first_row = jnp.clip(NA_QROWS * t - NA_KH // 2, 0, GRID_ROWS - NA_KROWS)
        start = pl.multiple_of(first_row * GRID_W, TQ_NA)
        q = q_ref[j * TQ_NA:(j + 1) * TQ_NA, :]
        s_loc = _dot_nt(q, kl_ref[pl.ds(start, TK_NA), :]) + bias_refs[j][0, 0]
        return start, [s_loc, _dot_nt(q, kc_ref[...])]

    nxt = scores(0)
    for j in range(NA_SUB):
        start, s = nxt
        if j + 1 < NA_SUB:
            nxt = scores(j + 1)
        o = _softmax_pv_blocks(s, [vl_ref[pl.ds(start, TK_NA), :], vc_ref[...]])
        o_ref[j * TQ_NA:(j + 1) * TQ_NA, :] = o.astype(BF16)


def _na_window_geometry(t):
    first_row = min(max(NA_QROWS * t - NA_KH // 2, 0), GRID_ROWS - NA_KROWS)
    r = NA_QROWS * t + np.arange(NA_QROWS)
    kr = first_row + np.arange(NA_KROWS)
    c = np.arange(GRID_W)
    kc = np.arange(GRID_W)
    r0 = np.clip(r - NA_KH // 2, 0, GRID_ROWS - NA_KH)
    c0 = np.clip(c - NA_KW // 2, 0, GRID_W - NA_KW)
    ok_r = (kr[None, :] >= r0[:, None]) & (kr[None, :] < r0[:, None] + NA_KH)
    ok_c = (kc[None, :] >= c0[:, None]) & (kc[None, :] < c0[:, None] + NA_KW)
    dr = np.clip(kr[None, :] - r[:, None] + NA_KH - 1, 0, 2 * NA_KH - 2)
    dc = np.clip(kc[None, :] - c[:, None] + NA_KW - 1, 0, 2 * NA_KW - 2)
    return dr, dc, ok_r, ok_c


def _na_bias_kernel(lo_ref, hi_ref, ok_ref, o_ref, *, row_offset):
    ok = ok_ref[...] > 0.0
    for v in range(3):
        for r in range(NA_QROWS):
            for pair in range(NA_KROWS // 2):
                d0 = row_offset[v][r][2 * pair]
                d1 = row_offset[v][r][2 * pair + 1]
                src = lo_ref[d0:d0 + 1, :] + hi_ref[d1:d1 + 1, :]
                block = pltpu.roll(jnp.broadcast_to(src, (GRID_W, 2 * GRID_W)), 0, 1,
                                   stride=1, stride_axis=0)
                o_ref[v, r * GRID_W:(r + 1) * GRID_W, pair * 2 * GRID_W:(pair + 1) * 2 * GRID_W] = (
                    jnp.where(ok, block, NEG_INF))


def _na_bias_table(na_rpb):
    n_dr, n_dc = 2 * NA_KH - 1, 2 * NA_KW - 1
    half = NA_KW - 1
    rpb = jnp.pad(na_rpb.astype(F32) * LOG2E, ((0, 0), (0, 0), (0, 1), (0, 0)),
                  constant_values=NEG_INF)
    lanes = 2 * GRID_W
    lo = jnp.roll(jnp.pad(rpb, ((0, 0),) * 3 + ((0, lanes - n_dc),)), -half, axis=-1)
    hi = jnp.pad(rpb, ((0, 0),) * 3 + ((GRID_W - half, lanes - n_dc - GRID_W + half),))
    geometry = [_na_window_geometry(t) for t in (0, 1, GRID_ROWS // NA_QROWS - 1)]
    ok_c = geometry[0][3]
    row_offset = [np.where(g[2], g[0], n_dr).tolist() for g in geometry]
    ok = jnp.asarray(np.concatenate([ok_c, ok_c], axis=1).astype(np.float32))
    src_spec = pl.BlockSpec((None, None, n_dr + 1, lanes), lambda l, h: (l, h, 0, 0))
    return _call(
        functools.partial(_na_bias_kernel, row_offset=row_offset), "na_bias",
        out_shape=jax.ShapeDtypeStruct((DEPTH, 3, 4, TQ_NA, TK_NA), F32),
        grid=(DEPTH, 4),
        in_specs=[src_spec, src_spec, pl.BlockSpec((GRID_W, lanes), lambda l, h: (0, 0))],
        out_specs=pl.BlockSpec((None, 3, None, TQ_NA, TK_NA), lambda l, h: (l, 0, h, 0, 0)),
    )(lo, hi, ok)


def _na_attention(qkv, qkv_c, bias, layer):
    tq = NA_SUB * TQ_NA
    nt = SEQ // tq
    kv = lambda col: pl.BlockSpec((SEQ, HEAD_DIM), lambda b, h, t: (b, col + h))
    kvc = lambda col: pl.BlockSpec((CTX_LEN, HEAD_DIM), lambda b, h, t: (b, col + h))
    bias_block = (None, 1, 1, TQ_NA, TK_NA)
    return _call(
        _na_kernel, "na_attention",
        out_shape=jax.ShapeDtypeStruct((BATCH * SEQ, 4 * HEAD_DIM), BF16),
        grid=(BATCH, 4, nt),
        in_specs=[
            pl.BlockSpec((tq, HEAD_DIM), lambda b, h, t: (b * nt + t, NA_Q + h)),
            kv(NA_K), kv(NA_V), kvc(NA_K), kvc(NA_V),
            pl.BlockSpec(bias_block, lambda b, h, t: (layer, jnp.where(t == 0, 0, 1), h, 0, 0)),
            pl.BlockSpec(bias_block, lambda b, h, t: (layer, 1, h, 0, 0)),
            pl.BlockSpec(bias_block, lambda b, h, t: (layer, jnp.where(t == nt - 1, 2, 1), h, 0, 0)),
        ],
        out_specs=pl.BlockSpec((tq, HEAD_DIM), lambda b, h, t: (b * nt + t, h)),
    )(qkv, qkv, qkv, qkv_c, qkv_c, bias, bias, bias)


def _gq_kernel(q_ref, kl_ref, vl_ref, kc_ref, vc_ref, o_ref, vaug):
    @pl.when(pl.program_id(2) == 0)
    def _():
        _fill_value_ones(vaug, vc_ref, vl_ref)

    for c in range(FLASH_SUB):
        rows = slice(c * TQ_FLASH, (c + 1) * TQ_FLASH)
        q = jnp.concatenate([q_ref[rows, 0:HEAD_DIM], q_ref[rows, HEAD_DIM:]], axis=0)
        acc = _online_softmax_pv(q, *_flash_chunks(kl_ref, kc_ref, vaug))
        o = acc[:, 0:HEAD_DIM] / acc[:, HEAD_DIM:]
        o_ref[rows, 0:HEAD_DIM] = o[0:TQ_FLASH].astype(BF16)
        o_ref[rows, HEAD_DIM:] = o[TQ_FLASH:].astype(BF16)


def _gq_attention(qkv, qkv_c):
    nt = SEQ // (FLASH_SUB * TQ_FLASH)
    kv = lambda col: pl.BlockSpec((SEQ, HEAD_DIM), lambda b, k, t: (b, col + k))
    kvc = lambda col: pl.BlockSpec((CTX_LEN, HEAD_DIM), lambda b, k, t: (b, col + k))
    return _call(
        _gq_kernel, "gq_attention",
        out_shape=jax.ShapeDtypeStruct((BATCH * SEQ, 4 * HEAD_DIM), BF16),
        grid=(BATCH, 2, nt),
        in_specs=[
            pl.BlockSpec((FLASH_SUB * TQ_FLASH,2 * HEAD_DIM), lambda b, k, t: (b * nt + t, GQ_Q // 2 + k)),
            kv(GQ_K), kv(GQ_V), kvc(GQ_K), kvc(GQ_V),
        ],
        out_specs=pl.BlockSpec((FLASH_SUB * TQ_FLASH,2 * HEAD_DIM), lambda b, k, t: (b * nt + t, k)),
        scratch_shapes=[pltpu.VMEM((CTX_LEN + SEQ, 2 * HEAD_DIM), BF16)],
    )(qkv, qkv, qkv, qkv_c, qkv_c)


def _df_kernel(lam_ref, subg_ref, q_ref, kl_ref, vl_ref, kc_ref, vc_ref, o_ref, vaug, *, lam_init):
    @pl.when(pl.program_id(2) == 0)
    def _():
        _fill_value_ones(vaug, vc_ref, vl_ref)

    lam = _diff_lambda(lam_ref, lam_init)
    for c in range(FLASH_SUB):
        rows = slice(c * TQ_FLASH, (c + 1) * TQ_FLASH)
        acc = _online_softmax_pv(_split_diff_query(q_ref[rows, :]),
                                 *_flash_chunks(kl_ref, kc_ref, vaug))
        o = acc[:, 0:HEAD_DIM] / acc[:, HEAD_DIM:]
        o = o[0:TQ_FLASH] - lam * o[TQ_FLASH:]
        o_ref[rows, :] = (_rms(o) * (subg_ref[...] * (1.0 - lam_init))).astype(BF16)


def _df_attention(qkv, qkv_c, df_lambda, subg, lam_init):
    nt = SEQ // (FLASH_SUB * TQ_FLASH)
    kv = lambda col: pl.BlockSpec((SEQ, HEAD_DIM), lambda b, h, t: (b, col + h))
    kvc = lambda col: pl.BlockSpec((CTX_LEN, HEAD_DIM), lambda b, h, t: (b, col + h))
    return _call(
        functools.partial(_df_kernel, lam_init=lam_init), "df_attention",
        out_shape=jax.ShapeDtypeStruct((BATCH * SEQ, 4 * HEAD_DIM), BF16),
        grid=(BATCH, 4, nt),
        in_specs=[
            pl.BlockSpec((4, DF_QK_DIM), lambda b, h, t: (0, 0)),
            pl.BlockSpec((1, HEAD_DIM), lambda b, h, t: (0, 0)),
            pl.BlockSpec((FLASH_SUB * TQ_FLASH,HEAD_DIM), lambda b, h, t: (b * nt + t, DF_Q + h)),
            kv(DF_K), kv(DF_V), kvc(DF_K), kvc(DF_V),
        ],
        out_specs=pl.BlockSpec((FLASH_SUB * TQ_FLASH,HEAD_DIM), lambda b, h, t: (b * nt + t, h)),
        scratch_shapes=[pltpu.VMEM((CTX_LEN + SEQ, 2 * HEAD_DIM), BF16)],
    )(df_lambda, subg, qkv, qkv, qkv, qkv_c, qkv_c)


def _sw_kernel(sink_ref, q_ref, kl_ref, vl_ref, kc_ref, vc_ref, m0_ref, m1_ref, m2_ref, o_ref):
    step = pl.program_id(2)
    mask_refs = [m0_ref] + [m1_ref] * (SW_SUB - 2) + [m2_ref]
    head = lax.broadcasted_iota(jnp.int32, (2 * TQ_SW, 1), 0) >= TQ_SW
    sink = jnp.where(head, sink_ref[0, 1:2, 0:1], sink_ref[0, 0:1, 0:1]) * LOG2E

    for j in range(SW_SUB):
        t = SW_SUB * step + j
        start = pl.multiple_of(jnp.clip(TQ_SW * t - SW_WINDOW, 0, SEQ - TK_SW), SW_WINDOW)
        keys = _rows(kl_ref[pl.ds(start, TK_SW), :], kc_ref[...])
        values = _rows(vl_ref[pl.ds(start, TK_SW), :], vc_ref[...])
        rows = slice(j * TQ_SW, (j + 1) * TQ_SW)
        q = _rows(q_ref[rows, 0:HEAD_DIM], q_ref[rows, HEAD_DIM:])
        o = _softmax_pv(_dot_nt(q, keys) + mask_refs[j][0], values, extra_logit=sink)
        o_ref[rows, 0:HEAD_DIM] = o[0:TQ_SW].astype(BF16)
        o_ref[rows, HEAD_DIM:] = o[TQ_SW:].astype(BF16)


def _sw_mask_table():
    r = np.arange(TQ_SW)[:, None]
    c = np.arange(TK_SW)[None, :]
    nt = SEQ // TQ_SW
    tables = []
    for t in (0, 1, nt - 1):
        start = min(max(TQ_SW * t - SW_WINDOW, 0), SEQ - TK_SW)
        ok = np.abs((TQ_SW * t + r) - (start + c)) <= SW_WINDOW
        band = np.where(ok, 0.0, NEG_INF).astype(np.float32)
        band = np.concatenate([band, np.zeros((TQ_SW, CTX_LEN), np.float32)], axis=1)
        tables.append(np.concatenate([band, band], axis=0))
    return jnp.asarray(np.stack(tables))


def _sw_attention(qkv, qkv_c, sink):
    tq = SW_SUB * TQ_SW
    nt = SEQ // tq
    mask = _sw_mask_table()
    kv = lambda col: pl.BlockSpec((SEQ, HEAD_DIM), lambda b, k, t: (b, col + k))
    kvc = lambda col: pl.BlockSpec((CTX_LEN, HEAD_DIM), lambda b, k, t: (b, col + k))
    mask_block = (1, 2 * TQ_SW, TK_SW + CTX_LEN)
    return _call(
        _sw_kernel, "sw_attention",
        out_shape=jax.ShapeDtypeStruct((BATCH * SEQ, 4 * HEAD_DIM), BF16),
        grid=(BATCH, 2, nt),
        in_specs=[
            pl.BlockSpec((1, 2, HEAD_DIM), lambda b, k, t: (k, 0, 0)),
            pl.BlockSpec((tq, 2 * HEAD_DIM), lambda b, k, t: (b * nt + t, SW_Q // 2 + k)),
            kv(SW_K), kv(SW_V), kvc(SW_K), kvc(SW_V),
            pl.BlockSpec(mask_block, lambda b, k, t: (jnp.where(t == 0, 0, 1), 0, 0)),
            pl.BlockSpec(mask_block, lambda b, k, t: (1, 0, 0)),
            pl.BlockSpec(mask_block, lambda b, k, t: (jnp.where(t == nt - 1, 2, 1), 0, 0)),
        ],
        out_specs=pl.BlockSpec((tq, 2 * HEAD_DIM), lambda b, k, t: (b * nt + t, k)),
    )(sink.reshape(2, 2, HEAD_DIM), qkv, qkv, qkv, qkv_c, qkv_c, mask, mask, mask)


def _ctx_attn_kernel(lam_ref, subg_ref, sink_ref, x_ref, o_ref, *, lam_init):
    chunk = lambda j: x_ref[:, j * HEAD_DIM:(j + 1) * HEAD_DIM]

    def put(j, o):
        o_ref[:, j * HEAD_DIM:(j + 1) * HEAD_DIM] = o.astype(BF16)

    lam = _diff_lambda(lam_ref, lam_init)
    for h in range(4):
        put(h, _softmax_pv(_dot_nt(chunk(NA_Q + h), chunk(NA_K + h)), chunk(NA_V + h)))
        put(4 + h, _softmax_pv(_dot_nt(chunk(GQ_Q + h), chunk(GQ_K + h // 2)),
                               chunk(GQ_V + h // 2)))
        q2 = _split_diff_query(chunk(DF_Q + h))
        o = _softmax_pv(_dot_nt(q2, chunk(DF_K + h)), chunk(DF_V + h))
        o = o[0:CTX_LEN] - lam * o[CTX_LEN:]
        put(8 + h, _rms(o) * (subg_ref[...] * (1.0 - lam_init)))
        put(12 + h, _softmax_pv(_dot_nt(chunk(SW_Q + h), chunk(SW_K + h // 2)),
                                chunk(SW_V + h // 2),
                                extra_logit=sink_ref[h:h + 1, 0:1] * LOG2E))


def _ctx_attention(qkv_c, df_lambda, subg, sink, lam_init):
    return _call(
        functools.partial(_ctx_attn_kernel, lam_init=lam_init), "ctx_attention",
        out_shape=jax.ShapeDtypeStruct((BATCH * CTX_LEN, D_MODEL), BF16),
        grid=(BATCH,),
        in_specs=[
            pl.BlockSpec((4, DF_QK_DIM), lambda b: (0, 0)),
            pl.BlockSpec((1, HEAD_DIM), lambda b: (0, 0)),
            pl.BlockSpec((4, HEAD_DIM), lambda b: (0, 0)),
            pl.BlockSpec((CTX_LEN, IN_WIDTH), lambda b: (b, 0)),
        ],
        out_specs=pl.BlockSpec((CTX_LEN, D_MODEL), lambda b: (b, 0)),
    )(df_lambda, subg, sink, qkv_c)


def _outproj_kernel(*refs, round_next_in_proj):
    if round_next_in_proj:
        (y0_ref, y1_ref, y2_ref, y3_ref, h_ref, mod_ref, g_ref, w32_ref, win_ref,
         o_ref, f_ref, win_out_ref, w_ref) = refs
        win_out_ref[...] = win_ref[...].astype(BF16)
    else:
        y0_ref, y1_ref, y2_ref, y3_ref, h_ref, mod_ref, g_ref, w32_ref, o_ref, f_ref, w_ref = refs

    @pl.when(pl.program_id(0) == 0)
    def _():
        w_ref[...] = w32_ref[...].astype(BF16)

    gate = mod_ref[0, 2:3, :] * g_ref[1:2, :]
    ffn_gain = _gain(g_ref[2:3, :], mod_ref[0, 4:5, :])
    for c in range(TM_PROJ // OUT_SUB):
        rows = slice(c * OUT_SUB, (c + 1) * OUT_SUB)
        y = jnp.concatenate([y_ref[rows, :] for y_ref in (y0_ref, y1_ref, y2_ref, y3_ref)], axis=1)
        h_new = h_ref[rows, :] + _rms(_dot(y, w_ref[...])) * gate
        o_ref[rows, :] = h_new
        f_ref[rows, :] = (_rms(h_new) * ffn_gain + mod_ref[0, 3:4, :]).astype(BF16)


def _out_projection(ys, y_cols, h, mod, mod_row, g, w_out, layer, w_in_next=None):
    rows = h.shape[0]
    tm = TM_PROJ
    n = rows // tm
    y_specs = [pl.BlockSpec((tm, MIX_COLS), lambda i, c=c: (i, c)) for c in y_cols]
    row_block = pl.BlockSpec((tm, D_MODEL), lambda i: (i, 0))
    in_specs = y_specs + [
        row_block,
        pl.BlockSpec((1, 6, D_MODEL), lambda i: (mod_row(i), 0, 0)),
        pl.BlockSpec((4, D_MODEL), lambda i: (0, 0)),
        _resident_layer(layer, (D_MODEL, D_MODEL)),
    ]
    args = [*ys, h, mod, g, w_out]
    out_shape = [jax.ShapeDtypeStruct((rows, D_MODEL), F32), jax.ShapeDtypeStruct((rows, D_MODEL), BF16)]
    out_specs = [row_block, row_block]
    if w_in_next is not None:
        n_slabs = 32
        slab = D_MODEL // n_slabs
        assert n_slabs <= n
        last = n_slabs - 1
        in_specs.append(pl.BlockSpec((None, slab, IN_WIDTH),
                                     lambda i: (layer + 1, jnp.minimum(i, last), 0)))
        args.append(w_in_next)
        out_shape.append(jax.ShapeDtypeStruct((D_MODEL, IN_WIDTH), BF16))
        out_specs.append(pl.BlockSpec((slab, IN_WIDTH), lambda i: (jnp.minimum(i, last), 0)))
    return _call(
        functools.partial(_outproj_kernel, round_next_in_proj=w_in_next is not None),
        "out_projection",
        out_shape=out_shape,
        grid=(n,),
        in_specs=in_specs,
        out_specs=out_specs,
        scratch_shapes=[pltpu.VMEM((D_MODEL, D_MODEL), BF16)],
    )(*args)


def _ffn_up_kernel(*refs, round_down_proj):
    if round_down_proj:
        f_ref, wg_ref, wu_ref, wd_ref, o_ref, wd_out_ref, wg_scr, wu_scr = refs
        wd_out_ref[...] = wd_ref[...].astype(BF16)
    else:
        f_ref, wg_ref, wu_ref, o_ref, wg_scr, wu_scr = refs

    @pl.when(pl.program_id(1) == 0)
    def _():
        wg_scr[...] = wg_ref[...].astype(BF16)
        wu_scr[...] = wu_ref[...].astype(BF16)

    for c in range(f_ref.shape[0] // UP_SUB):
        rows = slice(c * UP_SUB, (c + 1) * UP_SUB)
        f = f_ref[rows, :]
        gate = _dot(f, wg_scr[...])
        up = _dot(f, wu_scr[...])
        o_ref[rows, :] = (gate * jax.nn.sigmoid(gate) * up).astype(BF16)


def _ffn_up(f, w_gate_up, layer, w_down=None):
    rows = f.shape[0]
    tm, tf = min(TM_UP, rows), TF_UP
    nf, nt = D_FF // tf, rows // tm
    in_specs = [
        pl.BlockSpec((tm, D_MODEL), lambda k, i: (i, 0)),
        pl.BlockSpec((None, D_MODEL, tf), lambda k, i: (layer, 0, k)),
        pl.BlockSpec((None, D_MODEL, tf), lambda k, i: (layer, 0, nf + k)),
    ]
    args = [f, w_gate_up, w_gate_up]
    out_shape = [jax.ShapeDtypeStruct((rows, D_FF), BF16)]
    out_specs = [pl.BlockSpec((tm, tf), lambda k, i: (i, k))]
    if w_down is not None:
        slab = D_FF // (nf * nt)
        assert slab * nf * nt == D_FF and slab % 16 == 0
        in_specs.append(pl.BlockSpec((None, slab, D_MODEL), lambda k, i: (layer, k * nt + i, 0)))
        args.append(w_down)
        out_shape.append(jax.ShapeDtypeStruct((D_FF, D_MODEL), BF16))
        out_specs.append(pl.BlockSpec((slab, D_MODEL), lambda k, i: (k * nt + i, 0)))
    out = _call(
        functools.partial(_ffn_up_kernel, round_down_proj=w_down is not None), "ffn_up",
        out_shape=out_shape,
        grid=(nf, nt),
        in_specs=in_specs,
        out_specs=out_specs,
        scratch_shapes=[pltpu.VMEM((D_MODEL, tf), BF16), pltpu.VMEM((D_MODEL, tf), BF16)],
    )(*args)
    return out if w_down is not None else out[0]


def _ffn_down_kernel(a_ref, h_ref, mod_ref, g_ref, w_ref, o_ref):
    gate = mod_ref[0, 5:6, :] * g_ref[3:4, :]
    for c in range(TM_PROJ // ROW_SUB):
        rows = slice(c * ROW_SUB, (c + 1) * ROW_SUB)
        acc = _dot(a_ref[rows, :], w_ref[...])
        o_ref[rows, :] = h_ref[rows, :] + _rms(acc) * gate


def _ffn_down(act, h, mod, mod_row, g, w_down):
    rows = h.shape[0]
    tm = TM_PROJ
    row_block = pl.BlockSpec((tm, D_MODEL), lambda i: (i, 0))
    return _call(
        _ffn_down_kernel, "ffn_down",
        out_shape=jax.ShapeDtypeStruct((rows, D_MODEL), F32),
        grid=(rows // tm,),
        in_specs=[
            pl.BlockSpec((tm, D_FF), lambda i: (i, 0)),
            row_block,
            pl.BlockSpec((1, 6, D_MODEL), lambda i: (mod_row(i), 0, 0)),
            pl.BlockSpec((4, D_MODEL), lambda i: (0, 0)),
            _resident((D_FF, D_MODEL)),
        ],
        out_specs=row_block,
    )(act, h, mod, g, w_down)


def _rope_table(dim):
    q = dim // 4
    t = jnp.arange(SEQ)
    pos = jnp.stack([t // GRID_W, t % GRID_W], axis=-1).astype(F32)
    inv = ROPE_THETA ** (-jnp.arange(q, dtype=F32) / q)
    ang = pos[:, :, None] * inv
    cos = jnp.broadcast_to(jnp.cos(ang)[:, :, None, :], (SEQ, 2, 2, q)).reshape(SEQ, dim)
    sin = jnp.sin(ang)
    sin = jnp.stack([-sin, sin], axis=2).reshape(SEQ, dim)
    reps = HEAD_DIM // dim
    return jnp.tile(cos, (1, reps)), jnp.tile(sin, (1, reps))


def kernel(x, c, ctx, c_ctx, w_mod, b_mod, norm_g, w_in, w_out, na_rpb, qk_norm_g,
           df_lambda, df_subln_g, sw_sink, w_gate_up, w_down):
    pad_rows = jnp.zeros((MOD_ROWS - 1 - BATCH, D_MODEL), F32)
    cvec = jnp.concatenate([c_ctx[None, :], c, pad_rows], axis=0)
    mod_all = _modulation(cvec, w_mod, b_mod).reshape(DEPTH, MOD_ROWS, 6, D_MODEL)
    rope_tab = jnp.stack(_rope_table(HEAD_DIM) + _rope_table(DF_QK_DIM))
    na_bias = _na_bias_table(na_rpb)

    tiles_per_seq = SEQ // TM_PROJ
    lat_row = lambda i: 1 + i // tiles_per_seq
    ctx_row = lambda i: 0

    w_in_l = w_in[0].astype(BF16)

    h = x.reshape(BATCH * SEQ, D_MODEL)
    hc = ctx.reshape(BATCH * CTX_LEN, D_MODEL)
    for l in range(DEPTH):
        ctx_out = l < DEPTH - 1
        lam_init = 0.8 - 0.6 * math.exp(-0.3 * l)
        mod = mod_all[l]
        g = norm_g[l]
        subg = df_subln_g[l][None, :]
        sink = jnp.broadcast_to(sw_sink[l][:, None], (4, HEAD_DIM))

        qkv = _in_projection(h, mod, lat_row, g, qk_norm_g[l], w_in_l, rope_tab)
        qkv_c = _in_projection(hc, mod, ctx_row, g, qk_norm_g[l], w_in_l, None)

        ys = [
            _na_attention(qkv, qkv_c, na_bias, l),
            _gq_attention(qkv, qkv_c),
            _df_attention(qkv, qkv_c, df_lambda[l], subg, lam_init),
            _sw_attention(qkv, qkv_c, sink),
        ]
        if ctx_out:
            h, f, w_in_l = _out_projection(ys, (0, 0, 0, 0), h, mod, lat_row, g, w_out, l, w_in)
        else:
            h, f = _out_projection(ys, (0, 0, 0, 0), h, mod, lat_row, g, w_out, l)
        act, w_down_l = _ffn_up(f, w_gate_up, l, w_down)
        h = _ffn_down(act, h, mod, lat_row, g, w_down_l)
        if ctx_out:
            yc = _ctx_attention(qkv_c, df_lambda[l], subg, sink, lam_init)
            hc, fc = _out_projection([yc] * 4, (0, 1, 2, 3), hc, mod, ctx_row, g, w_out, l)
            hc = _ffn_down(_ffn_up(fc, w_gate_up, l), hc, mod, ctx_row, g, w_down_l)
    return h.reshape(BATCH, SEQ, D_MODEL)
```
